```python
import jax, jax.numpy as jnp
from jax import lax
import numpy as np

D_MODEL = 1024
BATCH = 1
SEQ = 16384
DEPTH = 4

N_MIXERS = 2
EPS = 1e-6
GLA_HEADS = 4
GLA_DK = D_MODEL // 2 // GLA_HEADS
GLA_DV = D_MODEL // GLA_HEADS
GLA_GATE_RANK = 16
GLA_GATE_NORM = 16.0
GLA_CHUNK = 64
GLA_IN = 2 * GLA_HEADS * GLA_DK + 2 * GLA_HEADS * GLA_DV + GLA_GATE_RANK
MOBA_HEADS = 8
MOBA_HD = D_MODEL // MOBA_HEADS
MOBA_BLOCK = 256
MOBA_TOPK = 3
MOBA_QCHUNK = 64
MOBA_IN = 3 * MOBA_HEADS * MOBA_HD
ROPE_THETA = 10000.0
D_FF = 4 * D_MODEL
N_GLA_LAYERS = (DEPTH + 1) // 2
N_MOBA_LAYERS = DEPTH // 2

kernel_name = "hybrid_gla_moba_adaln_trunk"


def rms_norm(t, g):
    tf = t.astype(jnp.float32)
    y = tf * lax.rsqrt(jnp.mean(tf * tf, axis=-1, keepdims=True) + EPS)
    return (y * g.astype(jnp.float32)).astype(t.dtype)


def rope(t, positions):
    half = t.shape[-1] // 2
    inv_freq = ROPE_THETA ** (-jnp.arange(half, dtype=jnp.float32) / half)
    ang = positions.astype(jnp.float32)[..., None] * inv_freq
    cos = jnp.cos(ang)[:, :, None, :]
    sin = jnp.sin(ang)[:, :, None, :]
    t1, t2 = t[..., :half], t[..., half:]
    return jnp.concatenate([t1 * cos - t2 * sin, t2 * cos + t1 * sin], axis=-1)


def gla_mixer(h, w_in, w_gate_up, b_gate, onorm_g, w_out):
    B, S, _ = h.shape
    H, DK, DV, L = GLA_HEADS, GLA_DK, GLA_DV, GLA_CHUNK
    NC = S // L
    f32 = jnp.float32
    splits = np.cumsum([H * DK, H * DK, H * DV, H * DV]).tolist()
    q, k, v, r, glr = jnp.split(h @ w_in, splits, axis=-1)
    g = jax.nn.log_sigmoid((glr @ w_gate_up + b_gate).astype(f32)) / GLA_GATE_NORM
    q = q.astype(f32).reshape(B, NC, L, H, DK) * (DK ** -0.5)
    k = k.astype(f32).reshape(B, NC, L, H, DK)
    v = v.astype(f32).reshape(B, NC, L, H, DV)
    G = jnp.cumsum(g.reshape(B, NC, L, H, DK), axis=2)
    G_last = G[:, :, -1]
    q_dec = q * jnp.exp(G)
    k_inv = k * jnp.exp(-G)
    causal = jnp.tril(jnp.ones((L, L), dtype=bool))
    A = jnp.where(causal, jnp.einsum('bnihd,bnjhd->bnhij', q_dec, k_inv), 0.0)
    o_intra = jnp.einsum('bnhij,bnjhv->bnihv', A, v)
    k_tail = k * jnp.exp(G_last[:, :, None] - G)
    U = jnp.einsum('bnjhd,bnjhv->bnhdv', k_tail, v)
    decay = jnp.exp(G_last)

    def step(state, inp):
        a, u = inp
        return a[..., None] * state + u, state

    _, states = lax.scan(step, jnp.zeros((B, H, DK, DV), f32),
                         (jnp.moveaxis(decay, 1, 0), jnp.moveaxis(U, 1, 0)))
    states = jnp.moveaxis(states, 0, 1)
    o_inter = jnp.einsum('bnihd,bnhdv->bnihv', q_dec, states)
    o = (o_intra + o_inter).reshape(B, S, H, DV)
    o = rms_norm(o, onorm_g).reshape(B, S, H * DV) * jax.nn.silu(r.astype(f32))
    return o.astype(h.dtype) @ w_out


def moba_mixer(h, positions, w_in, q_norm_g, k_norm_g, w_out):
    B, S, _ = h.shape
    H, HD, BS, C = MOBA_HEADS, MOBA_HD, MOBA_BLOCK, MOBA_QCHUNK
    f32 = jnp.float32
    q, k, v = jnp.split(h @ w_in, 3, axis=-1)
    q = rope(rms_norm(q.reshape(B, S, H, HD).astype(f32), q_norm_g), positions)
    k = rope(rms_norm(k.reshape(B, S, H, HD).astype(f32), k_norm_g), positions)
    v = v.reshape(B, S, H, HD).astype(f32)
    NB = -(-S // BS)
    Sp = NB * BS
    pad = ((0, 0), (0, Sp - S), (0, 0), (0, 0))
    q, k, v = jnp.pad(q, pad), jnp.pad(k, pad), jnp.pad(v, pad)
    k_blocks = k.reshape(B, NB, BS, H, HD).transpose(0, 3, 1, 2, 4)
    v_blocks = v.reshape(B, NB, BS, H, HD).transpose(0, 3, 1, 2, 4)
    k_mean = jnp.mean(k_blocks, axis=3)
    gate = jnp.einsum('bshd,bhnd->bshn', q, k_mean)
    own_blk = jnp.arange(Sp) // BS
    past = jnp.arange(NB)[None, :] < own_blk[:, None]
    gate = jnp.where(past[None, :, None, :], gate, -jnp.inf)
    topk = min(MOBA_TOPK, NB)
    _, sel = lax.top_k(gate, topk)
    valid = sel < own_blk[None, :, None, None]

    NQ = Sp // C

    def chunkify(t):
        return jnp.moveaxis(t.reshape(B, NQ, C, *t.shape[2:]), 1, 0)

    scale = HD ** -0.5
    b_idx = jnp.arange(B)[:, None, None, None]
    h_idx = jnp.arange(H)[None, None, :, None]
    q_off = jnp.arange(C)
    k_off = jnp.arange(BS)

    def attend(args):
        ci, qc, sc, vc = args
        s0 = ci * C
        start = (s0 // BS) * BS
        k_own = lax.dynamic_slice_in_dim(k, start, BS, axis=1)
        v_own = lax.dynamic_slice_in_dim(v, start, BS, axis=1)
        kg = k_blocks[b_idx, h_idx, sc]
        vg = v_blocks[b_idx, h_idx, sc]
        s_sel = jnp.einsum('bchd,bchjkd->bchjk', qc, kg) * scale
        s_sel = jnp.where(vc[..., None], s_sel, -jnp.inf).reshape(B, C, H, topk * BS)
        s_own = jnp.einsum('bchd,bkhd->bchk', qc, k_own) * scale
        causal = (start + k_off)[None, :] <= (s0 + q_off)[:, None]
        s_own = jnp.where(causal[None, :, None, :], s_own, -jnp.inf)
        p = jax.nn.softmax(jnp.concatenate([s_sel, s_own], axis=-1), axis=-1)
        p_sel = p[..., :topk * BS].reshape(B, C, H, topk, BS)
        p_own = p[..., topk * BS:]
        return (jnp.einsum('bchjk,bchjkd->bchd', p_sel, vg)
                + jnp.einsum('bchk,bkhd->bchd', p_own, v_own))

    out = lax.map(attend, (jnp.arange(NQ), chunkify(q), chunkify(sel), chunkify(valid)))
    out = jnp.moveaxis(out, 0, 1).reshape(B, Sp, H * HD)[:, :S]
    return out.astype(h.dtype) @ w_out


def sq_relu_mlp(h, w1, w2):
    return jnp.square(jax.nn.relu(h @ w1)) @ w2


def setup_inputs(seed: int = 0) -> dict:
    key = jax.random.key(seed)
    ks = jax.random.split(key, 20)
    D = D_MODEL
    nrm = lambda k, shape, fan_in: jax.random.normal(k, shape, jnp.float32) * (fan_in ** -0.5)
    x = jax.random.normal(ks[0], (BATCH, SEQ, D), jnp.float32)
    c = jax.random.normal(ks[1], (BATCH, D), jnp.float32)
    positions = jnp.broadcast_to(jnp.arange(SEQ, dtype=jnp.int32), (BATCH, SEQ))
    return {
        "x": x,
        "c": c,
        "positions": positions,
        "ada_w": nrm(ks[2], (DEPTH, D, 6 * D), D),
        "ada_b": 0.01 * jax.random.normal(ks[3], (DEPTH, 6 * D), jnp.float32),
        "norm_mix_g": 1.0 + 0.05 * jax.random.normal(ks[4], (DEPTH, D), jnp.float32),
        "norm_mlp_g": 1.0 + 0.05 * jax.random.normal(ks[5], (DEPTH, D), jnp.float32),
        "gla_w_in": nrm(ks[6], (N_GLA_LAYERS, D, GLA_IN), D),
        "gla_w_gate_up": nrm(ks[7], (N_GLA_LAYERS, GLA_GATE_RANK, GLA_HEADS * GLA_DK), GLA_GATE_RANK),
        "gla_b_gate": 0.01 * jax.random.normal(ks[8], (N_GLA_LAYERS, GLA_HEADS * GLA_DK), jnp.float32),
        "gla_onorm_g": 1.0 + 0.05 * jax.random.normal(ks[9], (N_GLA_LAYERS, GLA_DV), jnp.float32),
        "gla_w_out": nrm(ks[10], (N_GLA_LAYERS, GLA_HEADS * GLA_DV, D), GLA_HEADS * GLA_DV),
        "moba_w_in": nrm(ks[11], (N_MOBA_LAYERS, D, MOBA_IN), D),
        "moba_q_norm_g": 1.0 + 0.05 * jax.random.normal(ks[12], (N_MOBA_LAYERS, MOBA_HD), jnp.float32),
        "moba_k_norm_g": 1.0 + 0.05 * jax.random.normal(ks[13], (N_MOBA_LAYERS, MOBA_HD), jnp.float32),
        "moba_w_out": nrm(ks[14], (N_MOBA_LAYERS, MOBA_HEADS * MOBA_HD, D), MOBA_HEADS * MOBA_HD),
        "mlp_w1": nrm(ks[15], (DEPTH, D, D_FF), D),
        "mlp_w2": nrm(ks[16], (DEPTH, D_FF, D), D_FF),
    }


def reference(x, c, positions, ada_w, ada_b, norm_mix_g, norm_mlp_g,
              gla_w_in, gla_w_gate_up, gla_b_gate, gla_onorm_g, gla_w_out,
              moba_w_in, moba_q_norm_g, moba_k_norm_g, moba_w_out,
              mlp_w1, mlp_w2):
    c_act = jax.nn.silu(c)
    for i in range(DEPTH):
        mod = (c_act @ ada_w[i] + ada_b[i])[:, None, :]
        sh1, sc1, g1, sh2, sc2, g2 = jnp.split(mod, 6, axis=-1)
        h = rms_norm(x, norm_mix_g[i]) * (1 + sc1) + sh1
        j = i // N_MIXERS
        if i % N_MIXERS == 0:
            y = gla_mixer(h, gla_w_in[j], gla_w_gate_up[j], gla_b_gate[j],
                          gla_onorm_g[j], gla_w_out[j])
        else:
            y = moba_mixer(h, positions, moba_w_in[j], moba_q_norm_g[j],
                           moba_k_norm_g[j], moba_w_out[j])
        x = x + g1 * y
        h = rms_norm(x, norm_mlp_g[i]) * (1 + sc2) + sh2
        x = x + g2 * sq_relu_mlp(h, mlp_w1[i], mlp_w2[i])
    return x
```

```python
import functools

import jax
import jax.numpy as jnp
from jax import lax
from jax.experimental import pallas as pl
from jax.experimental.pallas import tpu as pltpu

f32 = jnp.float32
bf16 = jnp.bfloat16

EPS = 1e-6
GLA_HEADS = 4
GLA_GATE_RANK = 16
GLA_GATE_NORM = 16.0
GLA_CHUNK = 64
MOBA_HEADS = 8
MOBA_BLOCK = 256
MOBA_TOPK = 3
ROPE_THETA = 10000.0

LANES = 128
VMEM_LIMIT = 56 * 1024 * 1024

ROW_TILE = 512
GLA_STEP = 256
FF_CHUNK = 1024


def _params(*sem):
    return pltpu.CompilerParams(dimension_semantics=sem, vmem_limit_bytes=VMEM_LIMIT)


def _resident(shape):
    nd = len(shape)
    return pl.BlockSpec(shape, lambda *_: (0,) * nd, pipeline_mode=pl.Buffered(1))


def _dot(a, b):
    return jnp.dot(a, b, preferred_element_type=f32)


def _dot_nt(a, b):
    return lax.dot_general(a, b, (((1,), (1,)), ((), ())), preferred_element_type=f32)


def _split3(x):
    x1 = x.astype(bf16)
    e1 = x - x1.astype(f32)
    x2 = e1.astype(bf16)
    x3 = (e1 - x2.astype(f32)).astype(bf16)
    return x1, x2, x3


def _silu(x):
    return x / (1.0 + jnp.exp(-x))


def _norm_mod(x, g, scale, shift):
    ms = jnp.mean(x * x, axis=-1, keepdims=True)
    y = x * lax.rsqrt(ms + EPS)
    return (y * g) * (1.0 + scale) + shift


def _ada_kernel(c_ref, w_ref, b_ref, o_ref):
    ca = _silu(c_ref[...])
    o_ref[...] = jnp.sum(ca * w_ref[...], axis=0, keepdims=True) + b_ref[...]


def _ada_mod(c, ada_w, ada_b):
    depth, d, n = ada_w.shape
    tn = 1536
    out = pl.pallas_call(
        _ada_kernel,
        out_shape=jax.ShapeDtypeStruct((depth, 1, n), f32),
        grid=(depth, n // tn),
        in_specs=[
            pl.BlockSpec((d, 1), lambda l, j: (0, 0)),
            pl.BlockSpec((None, d, tn), lambda l, j: (l, 0, j)),
            pl.BlockSpec((None, 1, tn), lambda l, j: (l, 0, j)),
        ],
        out_specs=pl.BlockSpec((None, 1, tn), lambda l, j: (l, 0, j)),
        compiler_params=_params("arbitrary", "arbitrary"),
        name="ada_mod",
    )(c.reshape(d, 1), ada_w, ada_b.reshape(depth, 1, n))
    return out.reshape(depth, 6, d)


def _rope_table_kernel(pos_ref, freq_ref, sign_ref, cos_ref, sin_ref):
    ang = pos_ref[...].astype(f32) * freq_ref[...]
    cos_ref[...] = jnp.cos(ang)
    sin_ref[...] = jnp.sin(ang) * sign_ref[...]


def _rope_tables(positions, hd):
    s = positions.shape[-1]
    half = hd // 2
    inv_freq = ROPE_THETA ** (-jnp.arange(half, dtype=f32) / half)
    freq = jnp.concatenate([inv_freq, inv_freq]).reshape(1, hd)
    sign = jnp.concatenate([-jnp.ones((half,), f32), jnp.ones((half,), f32)]).reshape(1, hd)
    tm = ROW_TILE
    return pl.pallas_call(
        _rope_table_kernel,
        out_shape=(jax.ShapeDtypeStruct((s, hd), f32), jax.ShapeDtypeStruct((s, hd), f32)),
        grid=(s // tm,),
        in_specs=[
            pl.BlockSpec((tm, 1), lambda i: (i, 0)),
            pl.BlockSpec((1, hd), lambda i: (0, 0)),
            pl.BlockSpec((1, hd), lambda i: (0, 0)),
        ],
        out_specs=(pl.BlockSpec((tm, hd), lambda i: (i, 0)), pl.BlockSpec((tm, hd), lambda i: (i, 0))),
        compiler_params=_params("arbitrary"),
        name="rope_tables",
    )(positions.reshape(s, 1), freq, sign)


def _gla_in_kernel(x_ref, mod_ref, ng_ref, w_ref, wglr_ref, wup_hi_ref, wup_lo_ref, bg_ref,
                   q_ref, k_ref, v_ref, r_ref, g_ref, *, hdk, hdv):
    h = _norm_mod(x_ref[...], ng_ref[...], mod_ref[1:2, :], mod_ref[0:1, :]).astype(bf16)
    q_ref[...] = _dot(h, w_ref[:, 0:hdk])
    k_ref[...] = _dot(h, w_ref[:, hdk:2 * hdk])
    v_ref[...] = _dot(h, w_ref[:, 2 * hdk:2 * hdk + hdv])
    r_ref[...] = _dot(h, w_ref[:, 2 * hdk + hdv:2 * hdk + 2 * hdv])
    glr = _dot(h, wglr_ref[...])
    g_hi = glr.astype(bf16)
    g_lo = (glr - g_hi.astype(f32)).astype(bf16)
    z = (_dot(g_hi, wup_hi_ref[...]) + _dot(g_lo, wup_hi_ref[...]) + _dot(g_hi, wup_lo_ref[...])
         + bg_ref[...])
    log_sig = jnp.minimum(z, 0.0) - jnp.log1p(jnp.exp(-jnp.abs(z)))
    g_ref[...] = log_sig / GLA_GATE_NORM


def _gla_in(x, mod, ng, w_in, w_gate_up, b_gate):
    s, d = x.shape
    hdk = w_gate_up.shape[1]
    rank = w_gate_up.shape[0]
    hdv = (w_in.shape[1] - 2 * hdk - rank) // 2
    n_main = 2 * hdk + 2 * hdv
    w_main = w_in[:, :n_main].astype(bf16)
    w_glr = jnp.pad(w_in[:, n_main:], ((0, 0), (0, LANES - rank))).astype(bf16)
    wup = jnp.pad(w_gate_up, ((0, LANES - rank), (0, 0)))
    wup_hi = wup.astype(bf16)
    wup_lo = (wup - wup_hi.astype(f32)).astype(bf16)
    tm = ROW_TILE
    row = lambda n: pl.BlockSpec((tm, n), lambda i: (i, 0))
    return pl.pallas_call(
        functools.partial(_gla_in_kernel, hdk=hdk, hdv=hdv),
        out_shape=(jax.ShapeDtypeStruct((s, hdk), f32), jax.ShapeDtypeStruct((s, hdk), f32),
                   jax.ShapeDtypeStruct((s, hdv), f32), jax.ShapeDtypeStruct((s, hdv), f32),
                   jax.ShapeDtypeStruct((s, hdk), f32)),
        grid=(s // tm,),
        in_specs=[row(d), _resident(mod.shape), _resident(ng.shape), _resident(w_main.shape),
                  _resident(w_glr.shape), _resident(wup_hi.shape), _resident(wup_lo.shape),
                  _resident((1, hdk))],
        out_specs=(row(hdk), row(hdk), row(hdv), row(hdv), row(hdk)),
        compiler_params=_params("arbitrary"),
        name="gla_in_proj",
    )(x, mod, ng, w_main, w_glr, wup_hi, wup_lo, b_gate.reshape(1, hdk))


def _gla_core_kernel(q_ref, k_ref, g_ref, v_ref, r_ref, og_ref, o_ref, s_ref, *, heads):
    @pl.when(pl.program_id(0) == 0)
    def _():
        s_ref[...] = jnp.zeros_like(s_ref)

    t = q_ref.shape[0]
    dk = q_ref.shape[1] // heads
    dv = v_ref.shape[1] // heads
    lc = GLA_CHUNK
    nch = t // lc
    row = lax.broadcasted_iota(jnp.int32, (t, t), 0)
    col = lax.broadcasted_iota(jnp.int32, (t, t), 1)
    tri = jnp.logical_and(col <= row, (row // lc) == (col // lc))
    tri_b = jnp.where(tri, 1.0, 0.0).astype(bf16)
    lane_chunk = lax.broadcasted_iota(jnp.int32, (dk, t), 1) // lc
    scale = dk ** -0.5

    for h in range(heads):
        ks = slice(h * dk, (h + 1) * dk)
        vs = slice(h * dv, (h + 1) * dv)
        g1, g2, g3 = _split3(g_ref[:, ks])
        gc = _dot(tri_b, g1) + _dot(tri_b, g2) + _dot(tri_b, g3)
        g_last = jnp.concatenate(
            [jnp.broadcast_to(gc[c * lc + lc - 1:c * lc + lc, :], (lc, dk)) for c in range(nch)], axis=0)
        kh = k_ref[:, ks]
        q_dec = (q_ref[:, ks] * scale * jnp.exp(gc)).astype(bf16)
        k_inv = (kh * jnp.exp(-gc)).astype(bf16)
        k_tail_t = (kh * jnp.exp(g_last - gc)).T
        gc_t = gc.T
        vh = v_ref[:, vs].astype(bf16)
        a = jnp.where(tri, _dot_nt(q_dec, k_inv), 0.0).astype(bf16)
        o_intra = _dot(a, vh)
        state = s_ref[h]
        o_inter = []
        for c in range(nch):
            o_inter.append(_dot(q_dec[c * lc:(c + 1) * lc, :], state.astype(bf16)))
            kt_c = jnp.where(lane_chunk == c, k_tail_t, 0.0).astype(bf16)
            decay = jnp.exp(gc_t[:, c * lc + lc - 1:c * lc + lc])
            state = decay * state + _dot(kt_c, vh)
        s_ref[h] = state
        o = o_intra + jnp.concatenate(o_inter, axis=0)
        ms = jnp.mean(o * o, axis=-1, keepdims=True)
        on = o * lax.rsqrt(ms + EPS) * og_ref[...]
        o_ref[:, vs] = (on * _silu(r_ref[:, vs])).astype(bf16)


def _gla_core(q, k, g, v, r, onorm_g):
    s, hdk = q.shape
    hdv = v.shape[1]
    heads = GLA_HEADS
    t = GLA_STEP
    row = lambda n: pl.BlockSpec((t, n), lambda i: (i, 0))
    return pl.pallas_call(
        functools.partial(_gla_core_kernel, heads=heads),
        out_shape=jax.ShapeDtypeStruct((s, hdv), bf16),
        grid=(s // t,),
        in_specs=[row(hdk), row(hdk), row(hdk), row(hdv), row(hdv), _resident((1, hdv // heads))],
        out_specs=row(hdv),
        scratch_shapes=[pltpu.VMEM((heads, hdk // heads, hdv // heads), f32)],
        compiler_params=_params("arbitrary"),
        name="gla_core",
    )(q, k, g, v, r, onorm_g.reshape(1, hdv // heads))


def _moba_in_kernel(x_ref, mod_ref, ng_ref, w_ref, qg_ref, kg_ref, cos_ref, sin_ref,
                    q_ref, k_ref, vt_ref, kmean_ref, *, heads):
    h = _norm_mod(x_ref[...], ng_ref[...], mod_ref[1:2, :], mod_ref[0:1, :]).astype(bf16)
    tm = x_ref.shape[0]
    d_attn = q_ref.shape[1]
    hd = d_attn // heads
    cos = cos_ref[...]
    sin = sin_ref[...]

    def norm_rope(y, g):
        ms = jnp.mean(y * y, axis=-1, keepdims=True)
        yn = y * lax.rsqrt(ms + EPS) * g
        return yn * cos + pltpu.roll(yn, hd // 2, 1) * sin

    yq = _dot(h, w_ref[:, 0:d_attn])
    for hh in range(heads):
        hs = slice(hh * hd, (hh + 1) * hd)
        q_ref[:, hs] = norm_rope(yq[:, hs], qg_ref[...])
    yk = _dot(h, w_ref[:, d_attn:2 * d_attn])
    for hh in range(heads):
        hs = slice(hh * hd, (hh + 1) * hd)
        kr = norm_rope(yk[:, hs], kg_ref[...])
        k_ref[:, hs] = kr.astype(bf16)
        for b in range(tm // MOBA_BLOCK):
            kmean_ref[b, :, hs] = jnp.mean(kr[b * MOBA_BLOCK:(b + 1) * MOBA_BLOCK, :], axis=0, keepdims=True)
    yv = _dot(h, w_ref[:, 2 * d_attn:3 * d_attn])
    vt_ref[...] = yv.T.astype(bf16)


def _moba_in(x, mod, ng, w_in, qg, kg, cos_t, sin_t):
    s, d = x.shape
    heads = MOBA_HEADS
    d_attn = w_in.shape[1] // 3
    hd = d_attn // heads
    tm = ROW_TILE
    nb = s // MOBA_BLOCK
    row = lambda n: pl.BlockSpec((tm, n), lambda i: (i, 0))
    w = w_in.astype(bf16)
    return pl.pallas_call(
        functools.partial(_moba_in_kernel, heads=heads),
        out_shape=(jax.ShapeDtypeStruct((s, d_attn), f32), jax.ShapeDtypeStruct((s, d_attn), bf16),
                   jax.ShapeDtypeStruct((d_attn, s), bf16), jax.ShapeDtypeStruct((nb, 1, d_attn), f32)),
        grid=(s // tm,),
        in_specs=[row(d), _resident(mod.shape), _resident(ng.shape), _resident(w.shape),
                  _resident((1, hd)), _resident((1, hd)), row(hd), row(hd)],
        out_specs=(row(d_attn), row(d_attn), pl.BlockSpec((d_attn, tm), lambda i: (0, i)),
                   pl.BlockSpec((tm // MOBA_BLOCK, 1, d_attn), lambda i: (i, 0, 0))),
        compiler_params=_params("arbitrary"),
        name="moba_in_proj",
    )(x, mod, ng, w, qg.reshape(1, hd), kg.reshape(1, hd), cos_t, sin_t)


def _moba_attn_kernel(q_ref, k_ref, vt_ref, kmean_ref, o_ref, acc_ref, m_ref, l_ref, bias_ref):
    i = pl.program_id(1)
    tq, hd = q_ref.shape
    bs = MOBA_BLOCK
    nb = kmean_ref.shape[0]
    neg = jnp.float32(-jnp.inf)
    q = q_ref[...]

    gate = lax.dot_general(kmean_ref[...], q, (((1,), (1,)), ((), ())),
                           preferred_element_type=f32, precision=lax.Precision.HIGHEST)
    blk = lax.broadcasted_iota(jnp.int32, (nb, tq), 0)
    gate = jnp.where(blk < i, gate, neg)
    bias = jnp.full((nb, tq), neg, f32)
    for _ in range(MOBA_TOPK):
        top = jnp.max(gate, axis=0, keepdims=True)
        is_top = jnp.logical_and(gate == top, top > neg)
        first = jnp.min(jnp.where(is_top, blk, nb), axis=0, keepdims=True)
        pick = blk == first
        bias = jnp.where(pick, 0.0, bias)
        gate = jnp.where(pick, neg, gate)
    bias_ref[...] = bias

    qb = (q * (hd ** -0.5)).astype(bf16)

    own = pl.multiple_of(i * bs, bs)
    st = _dot_nt(k_ref[pl.ds(own, bs), :], qb)
    key_pos = lax.broadcasted_iota(jnp.int32, (bs, tq), 0)
    qry_pos = lax.broadcasted_iota(jnp.int32, (bs, tq), 1)
    st = jnp.where(key_pos <= qry_pos, st, neg)
    m0 = jnp.max(st, axis=0, keepdims=True)
    p = jnp.exp(st - m0)
    m_ref[...] = m0
    l_ref[...] = jnp.sum(p, axis=0, keepdims=True)
    acc_ref[...] = _dot(vt_ref[:, pl.ds(own, bs)], p.astype(bf16))

    def past_block(j, carry):
        off = pl.multiple_of(j * bs, bs)
        sj = _dot_nt(k_ref[pl.ds(off, bs), :], qb) + bias_ref[pl.ds(j, 1), :]
        m_old = m_ref[...]
        m_new = jnp.maximum(m_old, jnp.max(sj, axis=0, keepdims=True))
        alpha = jnp.exp(m_old - m_new)
        pj = jnp.exp(sj - m_new)
        l_ref[...] = alpha * l_ref[...] + jnp.sum(pj, axis=0, keepdims=True)
        acc_ref[...] = alpha * acc_ref[...] + _dot(vt_ref[:, pl.ds(off, bs)], pj.astype(bf16))
        m_ref[...] = m_new
        return carry

    lax.fori_loop(0, i, past_block, 0)
    o_ref[...] = (acc_ref[...] / l_ref[...]).T.astype(bf16)


def _moba_attn(q, k, vt, kmean):
    s, d_attn = q.shape
    heads = MOBA_HEADS
    hd = d_attn // heads
    tq = MOBA_BLOCK
    nb = s // MOBA_BLOCK
    return pl.pallas_call(
        _moba_attn_kernel,
        out_shape=jax.ShapeDtypeStruct((s, d_attn), bf16),
        grid=(heads, s // tq),
        in_specs=[
            pl.BlockSpec((tq, hd), lambda h, i: (i, h)),
            pl.BlockSpec((s, hd), lambda h, i: (0, h)),
            pl.BlockSpec((hd, s), lambda h, i: (h, 0)),
            pl.BlockSpec((nb, hd), lambda h, i: (0, h)),
        ],
        out_specs=pl.BlockSpec((tq, hd), lambda h, i: (i, h)),
        scratch_shapes=[pltpu.VMEM((hd, tq), f32), pltpu.VMEM((1, tq), f32), pltpu.VMEM((1, tq), f32),
                        pltpu.VMEM((nb, tq), f32)],
        compiler_params=_params("arbitrary", "arbitrary"),
        name="moba_attn",
    )(q, k, vt, kmean)


def _out_mlp_kernel(a_ref, x_ref, mod_ref, ng_ref, wo_ref, w1_ref, w2_ref, o_ref, acc_ref):
    x1 = x_ref[...] + mod_ref[2:3, :] * _dot(a_ref[...], wo_ref[...])
    h = _norm_mod(x1, ng_ref[...], mod_ref[4:5, :], mod_ref[3:4, :]).astype(bf16)
    d_ff = w1_ref.shape[1]
    for c in range(d_ff // FF_CHUNK):
        cs = slice(c * FF_CHUNK, (c + 1) * FF_CHUNK)
        u = jnp.maximum(_dot(h, w1_ref[:, cs]), 0.0)
        part = _dot((u * u).astype(bf16), w2_ref[cs, :])
        if c == 0:
            acc_ref[...] = part
        else:
            acc_ref[...] += part
    o_ref[...] = x1 + mod_ref[5:6, :] * acc_ref[...]


def _out_mlp(a, x, mod, ng, w_out, w1, w2):
    s, d = x.shape
    tm = ROW_TILE
    wo, w1b, w2b = w_out.astype(bf16), w1.astype(bf16), w2.astype(bf16)
    row = lambda n: pl.BlockSpec((tm, n), lambda i: (i, 0))
    return pl.pallas_call(
        _out_mlp_kernel,
        out_shape=jax.ShapeDtypeStruct((s, d), f32),
        grid=(s // tm,),
        in_specs=[row(a.shape[1]), row(d), _resident(mod.shape), _resident(ng.shape),
                  _resident(wo.shape), _resident(w1b.shape), _resident(w2b.shape)],
        out_specs=row(d),
        scratch_shapes=[pltpu.VMEM((tm, d), f32)],
        compiler_params=_params("arbitrary"),
        name="out_proj_mlp",
    )(a, x, mod, ng, wo, w1b, w2b)


def kernel(x, c, positions, ada_w, ada_b, norm_mix_g, norm_mlp_g, gla_w_in, gla_w_gate_up, gla_b_gate,
           gla_onorm_g, gla_w_out, moba_w_in, moba_q_norm_g, moba_k_norm_g, moba_w_out, mlp_w1, mlp_w2):
    b, s, d = x.shape
    assert b == 1 and s % ROW_TILE == 0 and s % MOBA_BLOCK == 0
    depth = ada_w.shape[0]
    xs = x.reshape(s, d)
    mod = _ada_mod(c, ada_w, ada_b)
    moba_hd = moba_w_out.shape[1] // MOBA_HEADS
    cos_t, sin_t = _rope_tables(positions, moba_hd)
    for i in range(depth):
        j = i // 2
        ng = norm_mix_g[i].reshape(1, d)
        if i % 2 == 0:
            q, k, v, r, g = _gla_in(xs, mod[i], ng, gla_w_in[j], gla_w_gate_up[j], gla_b_gate[j])
            a = _gla_core(q, k, g, v, r, gla_onorm_g[j])
            w_out = gla_w_out[j]
        else:
            q, k, vt, kmean = _moba_in(xs, mod[i], ng, moba_w_in[j], moba_q_norm_g[j], moba_k_norm_g[j],
                                       cos_t, sin_t)
            a = _moba_attn(q, k, vt, kmean.reshape(kmean.shape[0], kmean.shape[2]))
            w_out = moba_w_out[j]
        xs = _out_mlp(a, xs, mod[i], norm_mlp_g[i].reshape(1, d), w_out, mlp_w1[i], mlp_w2[i])
    return xs.reshape(b, s, d)
```

```python
import functools

import jax
import jax.numpy as jnp
from jax import lax
from jax.experimental import pallas as pl
from jax.experimental.pallas import tpu as pltpu

f32 = jnp.float32
bf16 = jnp.bfloat16

EPS = 1e-6
GLA_HEADS = 4
GLA_GATE_RANK = 16
GLA_GATE_NORM = 16.0
GLA_CHUNK = 64
MOBA_HEADS = 8
MOBA_BLOCK = 256
MOBA_TOPK = 3
MOBA_GROUP = 4
LOG2E = 1.4426950408889634
ROPE_THETA = 10000.0

LANES = 128
VMEM_LIMIT = 56 * 1024 * 1024

ROW_TILE = 512
GLA_STEP = 256
FF_CHUNK = 1024


def _params(*sem):
    return pltpu.CompilerParams(dimension_semantics=sem, vmem_limit_bytes=VMEM_LIMIT)


def _resident(shape):
    nd = len(shape)
    return pl.BlockSpec(shape, lambda *_: (0,) * nd, pipeline_mode=pl.Buffered(1))


def _dot(a, b):
    return jnp.dot(a, b, preferred_element_type=f32)


def _dot_nt(a, b):
    return lax.dot_general(a, b, (((1,), (1,)), ((), ())), preferred_element_type=f32)


def _split3(x):
    x1 = x.astype(bf16)
    e1 = x - x1.astype(f32)
    x2 = e1.astype(bf16)
    x3 = (e1 - x2.astype(f32)).astype(bf16)
    return x1, x2, x3


def _silu(x):
    return x / (1.0 + jnp.exp(-x))


def _norm_mod(x, g, scale, shift):
    ms = jnp.mean(x * x, axis=-1, keepdims=True)
    y = x * lax.rsqrt(ms + EPS)
    return (y * g) * (1.0 + scale) + shift


def _ada_kernel(c_ref, w_ref, b_ref, o_ref):
    ca = _silu(c_ref[...])
    o_ref[...] = jnp.sum(ca * w_ref[...], axis=0, keepdims=True) + b_ref[...]


def _ada_mod(c, ada_w, ada_b):
    depth, d, n = ada_w.shape
    tn = 1536
    out = pl.pallas_call(
        _ada_kernel,
        out_shape=jax.ShapeDtypeStruct((depth, 1, n), f32),
        grid=(depth, n // tn),
        in_specs=[
            pl.BlockSpec((d, 1), lambda l, j: (0, 0)),
            pl.BlockSpec((None, d, tn), lambda l, j: (l, 0, j)),
            pl.BlockSpec((None, 1, tn), lambda l, j: (l, 0, j)),
        ],
        out_specs=pl.BlockSpec((None, 1, tn), lambda l, j: (l, 0, j)),
        compiler_params=_params("arbitrary", "arbitrary"),
        name="ada_mod",
    )(c.reshape(d, 1), ada_w, ada_b.reshape(depth, 1, n))
    return out.reshape(depth, 6, d)


def _rope_table_kernel(pos_ref, freq_ref, sign_ref, cos_ref, sin_ref):
    ang = pos_ref[...].astype(f32) * freq_ref[...]
    cos_ref[...] = jnp.cos(ang)
    sin_ref[...] = jnp.sin(ang) * sign_ref[...]


def _rope_tables(positions, hd):
    s = positions.shape[-1]
    half = hd // 2
    inv_freq = ROPE_THETA ** (-jnp.arange(half, dtype=f32) / half)
    freq = jnp.concatenate([inv_freq, inv_freq]).reshape(1, hd)
    sign = jnp.concatenate([-jnp.ones((half,), f32), jnp.ones((half,), f32)]).reshape(1, hd)
    tm = ROW_TILE
    return pl.pallas_call(
        _rope_table_kernel,
        out_shape=(jax.ShapeDtypeStruct((s, hd), f32), jax.ShapeDtypeStruct((s, hd), f32)),
        grid=(s // tm,),
        in_specs=[
            pl.BlockSpec((tm, 1), lambda i: (i, 0)),
            pl.BlockSpec((1, hd), lambda i: (0, 0)),
            pl.BlockSpec((1, hd), lambda i: (0, 0)),
        ],
        out_specs=(pl.BlockSpec((tm, hd), lambda i: (i, 0)), pl.BlockSpec((tm, hd), lambda i: (i, 0))),
        compiler_params=_params("arbitrary"),
        name="rope_tables",
    )(positions.reshape(s, 1), freq, sign)


def _gla_in_kernel(x_ref, mod_ref, ng_ref, w_ref, wglr_ref, wup_hi_ref, wup_lo_ref, bg_ref,
                   q_ref, k_ref, v_ref, r_ref, g_ref, *, hdk, hdv):
    h = _norm_mod(x_ref[...], ng_ref[...], mod_ref[1:2, :], mod_ref[0:1, :]).astype(bf16)
    q_ref[...] = _dot(h, w_ref[:, 0:hdk])
    k_ref[...] = _dot(h, w_ref[:, hdk:2 * hdk])
    v_ref[...] = _dot(h, w_ref[:, 2 * hdk:2 * hdk + hdv])
    r_ref[...] = _dot(h, w_ref[:, 2 * hdk + hdv:2 * hdk + 2 * hdv])
    glr = _dot(h, wglr_ref[...])
    g_hi = glr.astype(bf16)
    g_lo = (glr - g_hi.astype(f32)).astype(bf16)
    z = (_dot(g_hi, wup_hi_ref[...]) + _dot(g_lo, wup_hi_ref[...]) + _dot(g_hi, wup_lo_ref[...])
         + bg_ref[...])
    log_sig = jnp.minimum(z, 0.0) - jnp.log1p(jnp.exp(-jnp.abs(z)))
    g_ref[...] = log_sig / GLA_GATE_NORM


def _gla_in(x, mod, ng, w_in, w_gate_up, b_gate):
    s, d = x.shape
    hdk = w_gate_up.shape[1]
    rank = w_gate_up.shape[0]
    hdv = (w_in.shape[1] - 2 * hdk - rank) // 2
    n_main = 2 * hdk + 2 * hdv
    w_main = w_in[:, :n_main].astype(bf16)
    w_glr = jnp.pad(w_in[:, n_main:], ((0, 0), (0, LANES - rank))).astype(bf16)
    wup = jnp.pad(w_gate_up, ((0, LANES - rank), (0, 0)))
    wup_hi = wup.astype(bf16)
    wup_lo = (wup - wup_hi.astype(f32)).astype(bf16)
    tm = ROW_TILE
    row = lambda n: pl.BlockSpec((tm, n), lambda i: (i, 0))
    return pl.pallas_call(
        functools.partial(_gla_in_kernel, hdk=hdk, hdv=hdv),
        out_shape=(jax.ShapeDtypeStruct((s, hdk), f32), jax.ShapeDtypeStruct((s, hdk), f32),
                   jax.ShapeDtypeStruct((s, hdv), f32), jax.ShapeDtypeStruct((s, hdv), f32),
                   jax.ShapeDtypeStruct((s, hdk), f32)),
        grid=(s // tm,),
        in_specs=[row(d), _resident(mod.shape), _resident(ng.shape), _resident(w_main.shape),
                  _resident(w_glr.shape), _resident(wup_hi.shape), _resident(wup_lo.shape),
                  _resident((1, hdk))],
        out_specs=(row(hdk), row(hdk), row(hdv), row(hdv), row(hdk)),
        compiler_params=_params("arbitrary"),
        name="gla_in_proj",
    )(x, mod, ng, w_main, w_glr, wup_hi, wup_lo, b_gate.reshape(1, hdk))


def _gla_core_kernel(q_ref, k_ref, g_ref, v_ref, r_ref, og_ref, o_ref, s_ref, *, heads):
    @pl.when(pl.program_id(0) == 0)
    def _():
        s_ref[...] = jnp.zeros_like(s_ref)

    t = q_ref.shape[0]
    dk = q_ref.shape[1] // heads
    dv = v_ref.shape[1] // heads
    lc = GLA_CHUNK
    nch = t // lc
    row = lax.broadcasted_iota(jnp.int32, (t, t), 0)
    col = lax.broadcasted_iota(jnp.int32, (t, t), 1)
    tri = jnp.logical_and(col <= row, (row // lc) == (col // lc))
    tri_b = jnp.where(tri, 1.0, 0.0).astype(bf16)
    lane_chunk = lax.broadcasted_iota(jnp.int32, (dk, t), 1) // lc
    scale = dk ** -0.5

    for h in range(heads):
        ks = slice(h * dk, (h + 1) * dk)
        vs = slice(h * dv, (h + 1) * dv)
        g1, g2, g3 = _split3(g_ref[:, ks])
        gc = _dot(tri_b, g1) + _dot(tri_b, g2) + _dot(tri_b, g3)
        g_last = jnp.concatenate(
            [jnp.broadcast_to(gc[c * lc + lc - 1:c * lc + lc, :], (lc, dk)) for c in range(nch)], axis=0)
        kh = k_ref[:, ks]
        q_dec = (q_ref[:, ks] * scale * jnp.exp(gc)).astype(bf16)
        k_inv = (kh * jnp.exp(-gc)).astype(bf16)
        k_tail_t = (kh * jnp.exp(g_last - gc)).T
        gc_t = gc.T
        vh = v_ref[:, vs].astype(bf16)
        a = jnp.where(tri, _dot_nt(q_dec, k_inv), 0.0).astype(bf16)
        o_intra = _dot(a, vh)
        state = s_ref[h]
        o_inter = []
        for c in range(nch):
            o_inter.append(_dot(q_dec[c * lc:(c + 1) * lc, :], state.astype(bf16)))
            kt_c = jnp.where(lane_chunk == c, k_tail_t, 0.0).astype(bf16)
            decay = jnp.exp(gc_t[:, c * lc + lc - 1:c * lc + lc])
            state = decay * state + _dot(kt_c, vh)
        s_ref[h] = state
        o = o_intra + jnp.concatenate(o_inter, axis=0)
        ms = jnp.mean(o * o, axis=-1, keepdims=True)
        on = o * lax.rsqrt(ms + EPS) * og_ref[...]
        o_ref[:, vs] = (on * _silu(r_ref[:, vs])).astype(bf16)


def _gla_core(q, k, g, v, r, onorm_g):
    s, hdk = q.shape
    hdv = v.shape[1]
    heads = GLA_HEADS
    t = GLA_STEP
    row = lambda n: pl.BlockSpec((t, n), lambda i: (i, 0))
    return pl.pallas_call(
        functools.partial(_gla_core_kernel, heads=heads),
        out_shape=jax.ShapeDtypeStruct((s, hdv), bf16),
        grid=(s // t,),
        in_specs=[row(hdk), row(hdk), row(hdk), row(hdv), row(hdv), _resident((1, hdv // heads))],
        out_specs=row(hdv),
        scratch_shapes=[pltpu.VMEM((heads, hdk // heads, hdv // heads), f32)],
        compiler_params=_params("arbitrary"),
        name="gla_core",
    )(q, k, g, v, r, onorm_g.reshape(1, hdv // heads))


def _moba_in_kernel(x_ref, mod_ref, ng_ref, w_ref, qg_ref, kg_ref, cos_ref, sin_ref,
                    q_ref, k_ref, vt_ref, kmean_ref, *, heads):
    h = _norm_mod(x_ref[...], ng_ref[...], mod_ref[1:2, :], mod_ref[0:1, :]).astype(bf16)
    tm = x_ref.shape[0]
    d_attn = q_ref.shape[1]
    hd = d_attn // heads
    cos = cos_ref[...]
    sin = sin_ref[...]

    def norm_rope(y, g):
        ms = jnp.mean(y * y, axis=-1, keepdims=True)
        yn = y * lax.rsqrt(ms + EPS) * g
        return yn * cos + pltpu.roll(yn, hd // 2, 1) * sin

    yq = _dot(h, w_ref[:, 0:d_attn])
    for hh in range(heads):
        hs = slice(hh * hd, (hh + 1) * hd)
        q_ref[:, hs] = norm_rope(yq[:, hs], qg_ref[...])
    yk = _dot(h, w_ref[:, d_attn:2 * d_attn])
    for hh in range(heads):
        hs = slice(hh * hd, (hh + 1) * hd)
        kr = norm_rope(yk[:, hs], kg_ref[...])
        k_ref[:, hs] = kr.astype(bf16)
        for b in range(tm // MOBA_BLOCK):
            kmean_ref[b, :, hs] = jnp.mean(kr[b * MOBA_BLOCK:(b + 1) * MOBA_BLOCK, :], axis=0, keepdims=True)
    yv = _dot(h, w_ref[:, 2 * d_attn:3 * d_attn])
    vt_ref[...] = yv.T.astype(bf16)


def _moba_in(x, mod, ng, w_in, qg, kg, cos_t, sin_t):
    s, d = x.shape
    heads = MOBA_HEADS
    d_attn = w_in.shape[1] // 3
    hd = d_attn // heads
    tm = ROW_TILE
    nb = s // MOBA_BLOCK
    row = lambda n: pl.BlockSpec((tm, n), lambda i: (i, 0))
    w = w_in.astype(bf16)
    return pl.pallas_call(
        functools.partial(_moba_in_kernel, heads=heads),
        out_shape=(jax.ShapeDtypeStruct((s, d_attn), f32), jax.ShapeDtypeStruct((s, d_attn), bf16),
                   jax.ShapeDtypeStruct((d_attn, s), bf16), jax.ShapeDtypeStruct((nb, 1, d_attn), f32)),
        grid=(s // tm,),
        in_specs=[row(d), _resident(mod.shape), _resident(ng.shape), _resident(w.shape),
                  _resident((1, hd)), _resident((1, hd)), row(hd), row(hd)],
        out_specs=(row(d_attn), row(d_attn), pl.BlockSpec((d_attn, tm), lambda i: (0, i)),
                   pl.BlockSpec((tm // MOBA_BLOCK, 1, d_attn), lambda i: (i, 0, 0))),
        compiler_params=_params("arbitrary"),
        name="moba_in_proj",
    )(x, mod, ng, w, qg.reshape(1, hd), kg.reshape(1, hd), cos_t, sin_t)


def _moba_attn_kernel(q_ref, k_ref, vt_ref, kmean_ref, o_ref, acc_ref, m_ref, l_ref, bias_ref, s_ref):
    i = pl.program_id(1)
    tq, hd = q_ref.shape
    bs = MOBA_BLOCK
    gk = MOBA_GROUP * bs
    nb = kmean_ref.shape[0]
    neg = jnp.float32(-jnp.inf)
    q = q_ref[...]

    gate = lax.dot_general(kmean_ref[...], q, (((1,), (1,)), ((), ())),
                           preferred_element_type=f32, precision=lax.Precision.HIGHEST)
    blk = lax.broadcasted_iota(jnp.int32, (nb, tq), 0)
    gate = jnp.where(blk < i, gate, neg)
    bias = jnp.full((nb, tq), neg, f32)
    for _ in range(MOBA_TOPK):
        top = jnp.max(gate, axis=0, keepdims=True)
        is_top = jnp.logical_and(gate == top, top > neg)
        first = jnp.min(jnp.where(is_top, blk, nb), axis=0, keepdims=True)
        pick = blk == first
        bias = jnp.where(pick, 0.0, bias)
        gate = jnp.where(pick, neg, gate)
    bias_ref[...] = bias

    qb = (q * (hd ** -0.5 * LOG2E)).astype(bf16)

    def scores(g):
        off = pl.multiple_of(g * gk, gk)
        return _dot_nt(k_ref[pl.ds(off, gk), :], qb)

    n_groups = (i + MOBA_GROUP - 1) // MOBA_GROUP

    @pl.when(n_groups > 0)
    def _():
        s_ref[0] = scores(0)

    own = pl.multiple_of(i * bs, bs)
    st = _dot_nt(k_ref[pl.ds(own, bs), :], qb)
    key_pos = lax.broadcasted_iota(jnp.int32, (bs, tq), 0)
    qry_pos = lax.broadcasted_iota(jnp.int32, (bs, tq), 1)
    st = jnp.where(key_pos <= qry_pos, st, neg)
    m0 = jnp.max(st, axis=0, keepdims=True)
    p = jnp.exp2(st - m0)
    m_ref[...] = m0
    l_ref[...] = jnp.sum(p, axis=0, keepdims=True)
    acc_ref[...] = _dot(vt_ref[:, pl.ds(own, bs)], p.astype(bf16))

    def past_group(g, carry):
        slot = g % 2
        s_ref[1 - slot] = scores(jnp.minimum(g + 1, n_groups - 1))
        parts = [s_ref[slot, r * bs:(r + 1) * bs, :] + bias_ref[pl.ds(g * MOBA_GROUP + r, 1), :]
                 for r in range(MOBA_GROUP)]
        m_old = m_ref[...]
        m_new = functools.reduce(jnp.maximum, [jnp.max(x, axis=0, keepdims=True) for x in parts] + [m_old])
        alpha = jnp.exp2(m_old - m_new)
        ps = [jnp.exp2(x - m_new) for x in parts]
        l_ref[...] = alpha * l_ref[...] + functools.reduce(
            jnp.add, [jnp.sum(x, axis=0, keepdims=True) for x in ps])
        pcat = jnp.concatenate([x.astype(bf16) for x in ps], axis=0)
        off = pl.multiple_of(g * gk, gk)
        acc_ref[...] = alpha * acc_ref[...] + _dot(vt_ref[:, pl.ds(off, gk)], pcat)
        m_ref[...] = m_new
        return carry

    lax.fori_loop(0, n_groups, past_group, 0)
    o_ref[...] = (acc_ref[...] / l_ref[...]).T.astype(bf16)


def _moba_attn(q, k, vt, kmean):
    s, d_attn = q.shape
    heads = MOBA_HEADS
    hd = d_attn // heads
    tq = MOBA_BLOCK
    nb = s // MOBA_BLOCK
    assert nb % MOBA_GROUP == 0
    return pl.pallas_call(
        _moba_attn_kernel,
        out_shape=jax.ShapeDtypeStruct((s, d_attn), bf16),
        grid=(heads, s // tq),
        in_specs=[
            pl.BlockSpec((tq, hd), lambda h, i: (i, h)),
            pl.BlockSpec((s, hd), lambda h, i: (0, h)),
            pl.BlockSpec((hd, s), lambda h, i: (h, 0)),
            pl.BlockSpec((nb, hd), lambda h, i: (0, h)),
        ],
        out_specs=pl.BlockSpec((tq, hd), lambda h, i: (i, h)),
        scratch_shapes=[pltpu.VMEM((hd, tq), f32), pltpu.VMEM((1, tq), f32), pltpu.VMEM((1, tq), f32),
                        pltpu.VMEM((nb, tq), f32), pltpu.VMEM((2, MOBA_GROUP * MOBA_BLOCK, tq), f32)],
        compiler_params=_params("arbitrary", "arbitrary"),
        name="moba_attn",
    )(q, k, vt, kmean)


def _out_mlp_kernel(a_ref, x_ref, mod_ref, ng_ref, wo_ref, w1_ref, w2_ref, o_ref, acc_ref):
    x1 = x_ref[...] + mod_ref[2:3, :] * _dot(a_ref[...], wo_ref[...])
    h = _norm_mod(x1, ng_ref[...], mod_ref[4:5, :], mod_ref[3:4, :]).astype(bf16)
    d_ff = w1_ref.shape[1]
    for c in range(d_ff // FF_CHUNK):
        cs = slice(c * FF_CHUNK, (c + 1) * FF_CHUNK)
        u = jnp.maximum(_dot(h, w1_ref[:, cs]), 0.0)
        part = _dot((u * u).astype(bf16), w2_ref[cs, :])
        if c == 0:
            acc_ref[...] = part
        else:
            acc_ref[...] += part
    o_ref[...] = x1 + mod_ref[5:6, :] * acc_ref[...]


def _out_mlp(a, x, mod, ng, w_out, w1, w2):
    s, d = x.shape
    tm = ROW_TILE
    wo, w1b, w2b = w_out.astype(bf16), w1.astype(bf16), w2.astype(bf16)
    row = lambda n: pl.BlockSpec((tm, n), lambda i: (i, 0))
    return pl.pallas_call(
        _out_mlp_kernel,
        out_shape=jax.ShapeDtypeStruct((s, d), f32),
        grid=(s // tm,),
        in_specs=[row(a.shape[1]), row(d), _resident(mod.shape), _resident(ng.shape),
                  _resident(wo.shape), _resident(w1b.shape), _resident(w2b.shape)],
        out_specs=row(d),
        scratch_shapes=[pltpu.VMEM((tm, d), f32)],
        compiler_params=_params("arbitrary"),
        name="out_proj_mlp",
    )(a, x, mod, ng, wo, w1b, w2b)


def kernel(x, c, positions, ada_w, ada_b, norm_mix_g, norm_mlp_g, gla_w_in, gla_w_gate_up, gla_b_gate,
           gla_onorm_g, gla_w_out, moba_w_in, moba_q_norm_g, moba_k_norm_g, moba_w_out, mlp_w1, mlp_w2):
    b, s, d = x.shape
    assert b == 1 and s % ROW_TILE == 0 and s % MOBA_BLOCK == 0
    depth = ada_w.shape[0]
    xs = x.reshape(s, d)
    mod = _ada_mod(c, ada_w, ada_b)
    moba_hd = moba_w_out.shape[1] // MOBA_HEADS
    cos_t, sin_t = _rope_tables(positions, moba_hd)
    for i in range(depth):
        j = i // 2
        ng = norm_mix_g[i].reshape(1, d)
        if i % 2 == 0:
            q, k, v, r, g = _gla_in(xs, mod[i], ng, gla_w_in[j], gla_w_gate_up[j], gla_b_gate[j])
            a = _gla_core(q, k, g, v, r, gla_onorm_g[j])
            w_out = gla_w_out[j]
        else:
            q, k, vt, kmean = _moba_in(xs, mod[i], ng, moba_w_in[j], moba_q_norm_g[j], moba_k_norm_g[j],
                                       cos_t, sin_t)
            a = _moba_attn(q, k, vt, kmean.reshape(kmean.shape[0], kmean.shape[2]))
            w_out = moba_w_out[j]
        xs = _out_mlp(a, xs, mod[i], norm_mlp_g[i].reshape(1, d), w_out, mlp_w1[i], mlp_w2[i])
    return xs.reshape(b, s, d)
```

```python
import functools

import jax
import jax.numpy as jnp
from jax import lax
from jax.experimental import pallas as pl
from jax.experimental.pallas import tpu as pltpu

f32 = jnp.float32
bf16 = jnp.bfloat16

EPS = 1e-6
GLA_HEADS = 4
GLA_GATE_RANK = 16
GLA_GATE_NORM = 16.0
GLA_CHUNK = 64
MOBA_HEADS = 8
MOBA_BLOCK = 256
MOBA_TOPK = 3
MOBA_GROUP = 4
MOBA_QTILE = 512
LOG2E = 1.4426950408889634
MOBA_MASK_VALUE = -2.0 ** 30
MOBA_FAST_RANGE = 64.0
ROPE_THETA = 10000.0

LANES = 128
VMEM_LIMIT = 56 * 1024 * 1024

ROW_TILE = 512
GLA_STEP = 256
FF_CHUNK = 1024


def _params(*sem):
    return pltpu.CompilerParams(dimension_semantics=sem, vmem_limit_bytes=VMEM_LIMIT)


def _resident(shape):
    nd = len(shape)
    return pl.BlockSpec(shape, lambda *_: (0,) * nd, pipeline_mode=pl.Buffered(1))


def _dot(a, b):
    return jnp.dot(a, b, preferred_element_type=f32)


def _dot_nt(a, b):
    return lax.dot_general(a, b, (((1,), (1,)), ((), ())), preferred_element_type=f32)


def _split3(x):
    x1 = x.astype(bf16)
    e1 = x - x1.astype(f32)
    x2 = e1.astype(bf16)
    x3 = (e1 - x2.astype(f32)).astype(bf16)
    return x1, x2, x3


def _silu(x):
    return x / (1.0 + jnp.exp(-x))


def _norm_mod(x, g, scale, shift):
    ms = jnp.mean(x * x, axis=-1, keepdims=True)
    y = x * lax.rsqrt(ms + EPS)
    return (y * g) * (1.0 + scale) + shift


def _ada_kernel(c_ref, w_ref, b_ref, o_ref):
    ca = _silu(c_ref[...])
    o_ref[...] = jnp.sum(ca * w_ref[...], axis=0, keepdims=True) + b_ref[...]


def _ada_mod(c, ada_w, ada_b):
    depth, d, n = ada_w.shape
    tn = 1536
    out = pl.pallas_call(
        _ada_kernel,
        out_shape=jax.ShapeDtypeStruct((depth, 1, n), f32),
        grid=(depth, n // tn),
        in_specs=[
            pl.BlockSpec((d, 1), lambda l, j: (0, 0)),
            pl.BlockSpec((None, d, tn), lambda l, j: (l, 0, j)),
            pl.BlockSpec((None, 1, tn), lambda l, j: (l, 0, j)),
        ],
        out_specs=pl.BlockSpec((None, 1, tn), lambda l, j: (l, 0, j)),
        compiler_params=_params("arbitrary", "arbitrary"),
        name="ada_mod",
    )(c.reshape(d, 1), ada_w, ada_b.reshape(depth, 1, n))
    return out.reshape(depth, 6, d)


def _rope_table_kernel(pos_ref, freq_ref, sign_ref, cos_ref, sin_ref):
    ang = pos_ref[...].astype(f32) * freq_ref[...]
    cos_ref[...] = jnp.cos(ang)
    sin_ref[...] = jnp.sin(ang) * sign_ref[...]


def _rope_tables(positions, hd):
    s = positions.shape[-1]
    half = hd // 2
    inv_freq = ROPE_THETA ** (-jnp.arange(half, dtype=f32) / half)
    freq = jnp.concatenate([inv_freq, inv_freq]).reshape(1, hd)
    sign = jnp.concatenate([-jnp.ones((half,), f32), jnp.ones((half,), f32)]).reshape(1, hd)
    tm = ROW_TILE
    return pl.pallas_call(
        _rope_table_kernel,
        out_shape=(jax.ShapeDtypeStruct((s, hd), f32), jax.ShapeDtypeStruct((s, hd), f32)),
        grid=(s // tm,),
        in_specs=[
            pl.BlockSpec((tm, 1), lambda i: (i, 0)),
            pl.BlockSpec((1, hd), lambda i: (0, 0)),
            pl.BlockSpec((1, hd), lambda i: (0, 0)),
        ],
        out_specs=(pl.BlockSpec((tm, hd), lambda i: (i, 0)), pl.BlockSpec((tm, hd), lambda i: (i, 0))),
        compiler_params=_params("arbitrary"),
        name="rope_tables",
    )(positions.reshape(s, 1), freq, sign)


def _gla_in_kernel(x_ref, mod_ref, ng_ref, w_ref, wglr_ref, wup_hi_ref, wup_lo_ref, bg_ref,
                   q_ref, k_ref, v_ref, r_ref, g_ref, *, hdk, hdv):
    h = _norm_mod(x_ref[...], ng_ref[...], mod_ref[1:2, :], mod_ref[0:1, :]).astype(bf16)
    q_ref[...] = _dot(h, w_ref[:, 0:hdk])
    k_ref[...] = _dot(h, w_ref[:, hdk:2 * hdk])
    v_ref[...] = _dot(h, w_ref[:, 2 * hdk:2 * hdk + hdv])
    r_ref[...] = _dot(h, w_ref[:, 2 * hdk + hdv:2 * hdk + 2 * hdv])
    glr = _dot(h, wglr_ref[...])
    g_hi = glr.astype(bf16)
    g_lo = (glr - g_hi.astype(f32)).astype(bf16)
    z = (_dot(g_hi, wup_hi_ref[...]) + _dot(g_lo, wup_hi_ref[...]) + _dot(g_hi, wup_lo_ref[...])
         + bg_ref[...])
    log_sig = jnp.minimum(z, 0.0) - jnp.log1p(jnp.exp(-jnp.abs(z)))
    g_ref[...] = log_sig / GLA_GATE_NORM


def _gla_in(x, mod, ng, w_in, w_gate_up, b_gate):
    s, d = x.shape
    hdk = w_gate_up.shape[1]
    rank = w_gate_up.shape[0]
    hdv = (w_in.shape[1] - 2 * hdk - rank) // 2
    n_main = 2 * hdk + 2 * hdv
    w_main = w_in[:, :n_main].astype(bf16)
    w_glr = jnp.pad(w_in[:, n_main:], ((0, 0), (0, LANES - rank))).astype(bf16)
    wup = jnp.pad(w_gate_up, ((0, LANES - rank), (0, 0)))
    wup_hi = wup.astype(bf16)
    wup_lo = (wup - wup_hi.astype(f32)).astype(bf16)
    tm = ROW_TILE
    row = lambda n: pl.BlockSpec((tm, n), lambda i: (i, 0))
    return pl.pallas_call(
        functools.partial(_gla_in_kernel, hdk=hdk, hdv=hdv),
        out_shape=(jax.ShapeDtypeStruct((s, hdk), f32), jax.ShapeDtypeStruct((s, hdk), f32),
                   jax.ShapeDtypeStruct((s, hdv), f32), jax.ShapeDtypeStruct((s, hdv), f32),
                   jax.ShapeDtypeStruct((s, hdk), f32)),
        grid=(s // tm,),
        in_specs=[row(d), _resident(mod.shape), _resident(ng.shape), _resident(w_main.shape),
                  _resident(w_glr.shape), _resident(wup_hi.shape), _resident(wup_lo.shape),
                  _resident((1, hdk))],
        out_specs=(row(hdk), row(hdk), row(hdv), row(hdv), row(hdk)),
        compiler_params=_params("arbitrary"),
        name="gla_in_proj",
    )(x, mod, ng, w_main, w_glr, wup_hi, wup_lo, b_gate.reshape(1, hdk))


def _gla_core_kernel(q_ref, k_ref, g_ref, v_ref, r_ref, og_ref, o_ref, s_ref, *, heads):
    @pl.when(pl.program_id(0) == 0)
    def _():
        s_ref[...] = jnp.zeros_like(s_ref)

    t = q_ref.shape[0]
    dk = q_ref.shape[1] // heads
    dv = v_ref.shape[1] // heads
    lc = GLA_CHUNK
    nch = t // lc
    row = lax.broadcasted_iota(jnp.int32, (t, t), 0)
    col = lax.broadcasted_iota(jnp.int32, (t, t), 1)
    tri = jnp.logical_and(col <= row, (row // lc) == (col // lc))
    tri_b = jnp.where(tri, 1.0, 0.0).astype(bf16)
    lane_chunk = lax.broadcasted_iota(jnp.int32, (dk, t), 1) // lc
    scale = dk ** -0.5

    for h in range(heads):
        ks = slice(h * dk, (h + 1) * dk)
        vs = slice(h * dv, (h + 1) * dv)
        g1, g2, g3 = _split3(g_ref[:, ks])
        gc = _dot(tri_b, g1) + _dot(tri_b, g2) + _dot(tri_b, g3)
        g_last = jnp.concatenate(
            [jnp.broadcast_to(gc[c * lc + lc - 1:c * lc + lc, :], (lc, dk)) for c in range(nch)], axis=0)
        kh = k_ref[:, ks]
        q_dec = (q_ref[:, ks] * scale * jnp.exp(gc)).astype(bf16)
        k_inv = (kh * jnp.exp(-gc)).astype(bf16)
        k_tail_t = (kh * jnp.exp(g_last - gc)).T
        gc_t = gc.T
        vh = v_ref[:, vs].astype(bf16)
        a = jnp.where(tri, _dot_nt(q_dec, k_inv), 0.0).astype(bf16)
        o_intra = _dot(a, vh)
        state = s_ref[h]
        o_inter = []
        for c in range(nch):
            o_inter.append(_dot(q_dec[c * lc:(c + 1) * lc, :], state.astype(bf16)))
            kt_c = jnp.where(lane_chunk == c, k_tail_t, 0.0).astype(bf16)
            decay = jnp.exp(gc_t[:, c * lc + lc - 1:c * lc + lc])
            state = decay * state + _dot(kt_c, vh)
        s_ref[h] = state
        o = o_intra + jnp.concatenate(o_inter, axis=0)
        ms = jnp.mean(o * o, axis=-1, keepdims=True)
        on = o * lax.rsqrt(ms + EPS) * og_ref[...]
        o_ref[:, vs] = (on * _silu(r_ref[:, vs])).astype(bf16)


def _gla_core(q, k, g, v, r, onorm_g):
    s, hdk = q.shape
    hdv = v.shape[1]
    heads = GLA_HEADS
    t = GLA_STEP
    row = lambda n: pl.BlockSpec((t, n), lambda i: (i, 0))
    return pl.pallas_call(
        functools.partial(_gla_core_kernel, heads=heads),
        out_shape=jax.ShapeDtypeStruct((s, hdv), bf16),
        grid=(s // t,),
        in_specs=[row(hdk), row(hdk), row(hdk), row(hdv), row(hdv), _resident((1, hdv // heads))],
        out_specs=row(hdv),
        scratch_shapes=[pltpu.VMEM((heads, hdk // heads, hdv // heads), f32)],
        compiler_params=_params("arbitrary"),
        name="gla_core",
    )(q, k, g, v, r, onorm_g.reshape(1, hdv // heads))


def _moba_in_kernel(x_ref, mod_ref, ng_ref, w_ref, qg_ref, kg_ref, cos_ref, sin_ref,
                    qt_ref, k_ref, vt_ref, kmean_ref, *, heads):
    h = _norm_mod(x_ref[...], ng_ref[...], mod_ref[1:2, :], mod_ref[0:1, :]).astype(bf16)
    tm = x_ref.shape[0]
    d_attn = qt_ref.shape[0]
    hd = d_attn // heads
    nb = pl.num_programs(0) * (tm // MOBA_BLOCK)
    cos = cos_ref[...]
    sin = sin_ref[...]

    def norm_rope(y, g):
        ms = jnp.mean(y * y, axis=-1, keepdims=True)
        yn = y * lax.rsqrt(ms + EPS) * g
        return yn * cos + pltpu.roll(yn, hd // 2, 1) * sin

    row_blk = pl.program_id(0) * (tm // MOBA_BLOCK) + lax.broadcasted_iota(jnp.int32, (tm, hd), 0) // MOBA_BLOCK
    col = lax.broadcasted_iota(jnp.int32, (tm, hd), 1)
    extra = jnp.where(jnp.logical_or(col == row_blk, col == nb), 1.0, 0.0).astype(bf16)

    yq = _dot(h, w_ref[:, 0:d_attn])
    for hh in range(heads):
        hs = slice(hh * hd, (hh + 1) * hd)
        qt_ref[hs, :] = norm_rope(yq[:, hs], qg_ref[...]).T
    yk = _dot(h, w_ref[:, d_attn:2 * d_attn])
    for hh in range(heads):
        hs = slice(hh * hd, (hh + 1) * hd)
        kr = norm_rope(yk[:, hs], kg_ref[...])
        k_ref[:, 2 * hh * hd:(2 * hh + 1) * hd] = kr.astype(bf16)
        k_ref[:, (2 * hh + 1) * hd:(2 * hh + 2) * hd] = extra
        for b in range(tm // MOBA_BLOCK):
            kmean_ref[b, :, hs] = jnp.mean(kr[b * MOBA_BLOCK:(b + 1) * MOBA_BLOCK, :], axis=0, keepdims=True)
    yv = _dot(h, w_ref[:, 2 * d_attn:3 * d_attn])
    vt_ref[...] = yv.T.astype(bf16)


def _moba_in(x, mod, ng, w_in, qg, kg, cos_t, sin_t):
    s, d = x.shape
    heads = MOBA_HEADS
    d_attn = w_in.shape[1] // 3
    hd = d_attn // heads
    tm = ROW_TILE
    nb = s // MOBA_BLOCK
    assert nb + 1 <= hd
    row = lambda n: pl.BlockSpec((tm, n), lambda i: (i, 0))
    col = lambda n: pl.BlockSpec((n, tm), lambda i: (0, i))
    w = w_in.astype(bf16)
    return pl.pallas_call(
        functools.partial(_moba_in_kernel, heads=heads),
        out_shape=(jax.ShapeDtypeStruct((d_attn, s), f32), jax.ShapeDtypeStruct((s, 2 * d_attn), bf16),
                   jax.ShapeDtypeStruct((d_attn, s), bf16), jax.ShapeDtypeStruct((nb, 1, d_attn), f32)),
        grid=(s // tm,),
        in_specs=[row(d), _resident(mod.shape), _resident(ng.shape), _resident(w.shape),
                  _resident((1, hd)), _resident((1, hd)), row(hd), row(hd)],
        out_specs=(col(d_attn), row(2 * d_attn), col(d_attn),
                   pl.BlockSpec((tm // MOBA_BLOCK, 1, d_attn), lambda i: (i, 0, 0))),
        compiler_params=_params("arbitrary"),
        name="moba_in_proj",
    )(x, mod, ng, w, qg.reshape(1, hd), kg.reshape(1, hd), cos_t, sin_t)


def _moba_gate_bias(kmean_ref, qt, own_blk, off_value):
    nb = kmean_ref.shape[0]
    tq = qt.shape[1]
    neg = jnp.float32(-jnp.inf)
    gate = jnp.dot(kmean_ref[...], qt, preferred_element_type=f32, precision=lax.Precision.HIGHEST)
    blk = lax.broadcasted_iota(jnp.int32, (nb, tq), 0)
    gate = jnp.where(blk < own_blk, gate, neg)
    bias = jnp.full((nb, tq), off_value, f32)
    for _ in range(MOBA_TOPK):
        top = jnp.max(gate, axis=0, keepdims=True)
        is_top = jnp.logical_and(gate == top, top > neg)
        first = jnp.min(jnp.where(is_top, blk, nb), axis=0, keepdims=True)
        pick = blk == first
        bias = jnp.where(pick, 0.0, bias)
        gate = jnp.where(pick, neg, gate)
    return bias


def _causal_own_scores(k_ref, qb, blk):
    bs = MOBA_BLOCK
    hd = qb.shape[0]
    st = _dot(k_ref[pl.ds(pl.multiple_of(blk * bs, bs), bs), 0:hd], qb)
    key_pos = lax.broadcasted_iota(jnp.int32, (bs, bs), 0)
    qry_pos = lax.broadcasted_iota(jnp.int32, (bs, bs), 1)
    return jnp.where(key_pos <= qry_pos, st, jnp.float32(-jnp.inf))


def _moba_attn_fast_kernel(qt_ref, k_ref, vt_ref, kmean_ref, o_ref, acc_ref, l_ref, qaug_ref):
    hd, tq = qt_ref.shape
    bs = MOBA_BLOCK
    nt = tq // bs
    gk = MOBA_GROUP * bs
    nb = kmean_ref.shape[0]
    first = pl.program_id(1) * nt
    qt = qt_ref[...]
    own_blk = first + lax.broadcasted_iota(jnp.int32, (1, tq), 1) // bs
    bias = _moba_gate_bias(kmean_ref, qt, own_blk, MOBA_MASK_VALUE)
    qb = (qt * (hd ** -0.5 * LOG2E)).astype(bf16)

    shifts = []
    for a in range(nt):
        ls = slice(a * bs, (a + 1) * bs)
        st = _causal_own_scores(k_ref, qb[:, ls], first + a)
        c = jnp.max(st, axis=0, keepdims=True).astype(bf16).astype(f32)
        p = jnp.exp2(st - c)
        l_ref[:, ls] = jnp.sum(p, axis=0, keepdims=True)
        own = pl.multiple_of((first + a) * bs, bs)
        acc_ref[:, ls] = _dot(vt_ref[:, pl.ds(own, bs)], p.astype(bf16))
        shifts.append(c)
    c = jnp.concatenate(shifts, axis=1)

    qaug_ref[0:hd, :] = qb
    qaug_ref[hd:hd + nb, :] = bias.astype(bf16)
    tail = lax.broadcasted_iota(jnp.int32, (hd - nb, tq), 0)
    qaug_ref[hd + nb:2 * hd, :] = jnp.where(tail == 0, jnp.broadcast_to(-c, tail.shape), 0.0).astype(bf16)

    def past_group(g, carry):
        off = pl.multiple_of(g * gk, gk)
        pg = jnp.exp2(_dot(k_ref[pl.ds(off, gk), :], qaug_ref[...]))
        l_ref[...] += jnp.sum(pg, axis=0, keepdims=True)
        acc_ref[...] += _dot(vt_ref[:, pl.ds(off, gk)], pg.astype(bf16))
        return carry

    n_past = first + nt - 1
    lax.fori_loop(0, (n_past + MOBA_GROUP - 1) // MOBA_GROUP, past_group, 0)
    o_ref[...] = (acc_ref[...] / l_ref[...]).T.astype(bf16)


def _moba_attn_general_kernel(qt_ref, k_ref, vt_ref, kmean_ref, o_ref, acc_ref, l_ref, m_ref, bias_ref):
    hd, tq = qt_ref.shape
    bs = MOBA_BLOCK
    nt = tq // bs
    for a in range(nt):
        ls = slice(a * bs, (a + 1) * bs)
        i = pl.program_id(1) * nt + a
        qt = qt_ref[:, ls]
        bias_ref[...] = _moba_gate_bias(kmean_ref, qt, i, -jnp.inf)
        qb = (qt * (hd ** -0.5 * LOG2E)).astype(bf16)
        st = _causal_own_scores(k_ref, qb, i)
        m0 = jnp.max(st, axis=0, keepdims=True)
        p = jnp.exp2(st - m0)
        m_ref[...] = m0
        l_ref[:, ls] = jnp.sum(p, axis=0, keepdims=True)
        own = pl.multiple_of(i * bs, bs)
        acc_ref[:, ls] = _dot(vt_ref[:, pl.ds(own, bs)], p.astype(bf16))

        def past_block(j, carry, ls=ls, qb=qb):
            off = pl.multiple_of(j * bs, bs)
            sj = _dot(k_ref[pl.ds(off, bs), 0:hd], qb) + bias_ref[pl.ds(j, 1), :]
            m_old = m_ref[...]
            m_new = jnp.maximum(m_old, jnp.max(sj, axis=0, keepdims=True))
            alpha = jnp.exp2(m_old - m_new)
            pj = jnp.exp2(sj - m_new)
            l_ref[:, ls] = alpha * l_ref[:, ls] + jnp.sum(pj, axis=0, keepdims=True)
            acc_ref[:, ls] = alpha * acc_ref[:, ls] + _dot(vt_ref[:, pl.ds(off, bs)], pj.astype(bf16))
            m_ref[...] = m_new
            return carry

        lax.fori_loop(0, i, past_block, 0)
        o_ref[ls, :] = (acc_ref[:, ls] / l_ref[:, ls]).T.astype(bf16)


def _moba_attn(qt, k, vt, kmean, qg, kg):
    d_attn, s = qt.shape
    heads = MOBA_HEADS
    hd = d_attn // heads
    tq = MOBA_QTILE
    bs = MOBA_BLOCK
    nb = s // bs
    assert nb % MOBA_GROUP == 0 and s % tq == 0 and tq % bs == 0

    def call(body, scratch):
        return pl.pallas_call(
            body,
            out_shape=jax.ShapeDtypeStruct((s, d_attn), bf16),
            grid=(heads, s // tq),
            in_specs=[
                pl.BlockSpec((hd, tq), lambda h, t: (h, t)),
                pl.BlockSpec((s, 2 * hd), lambda h, t: (0, h)),
                pl.BlockSpec((hd, s), lambda h, t: (h, 0)),
                pl.BlockSpec((nb, hd), lambda h, t: (0, h)),
            ],
            out_specs=pl.BlockSpec((tq, hd), lambda h, t: (t, h)),
            scratch_shapes=scratch,
            compiler_params=_params("arbitrary", "arbitrary"),
            name=body.__name__.strip("_"),
        )(qt, k, vt, kmean)

    stats = [pltpu.VMEM((hd, tq), f32), pltpu.VMEM((1, tq), f32)]
    fast = lambda: call(_moba_attn_fast_kernel, stats + [pltpu.VMEM((2 * hd, tq), bf16)])
    general = lambda: call(_moba_attn_general_kernel,
                           stats + [pltpu.VMEM((1, bs), f32), pltpu.VMEM((nb, bs), f32)])
    bound = 1.02 * hd ** 0.5 * LOG2E * jnp.max(jnp.abs(qg)) * jnp.max(jnp.abs(kg))
    return lax.cond(2.0 * bound <= MOBA_FAST_RANGE, fast, general)


def _out_mlp_kernel(a_ref, x_ref, mod_ref, ng_ref, wo_ref, w1_ref, w2_ref, o_ref, acc_ref):
    x1 = x_ref[...] + mod_ref[2:3, :] * _dot(a_ref[...], wo_ref[...])
    h = _norm_mod(x1, ng_ref[...], mod_ref[4:5, :], mod_ref[3:4, :]).astype(bf16)
    d_ff = w1_ref.shape[1]
    for c in range(d_ff // FF_CHUNK):
        cs = slice(c * FF_CHUNK, (c + 1) * FF_CHUNK)
        u = jnp.maximum(_dot(h, w1_ref[:, cs]), 0.0)
        part = _dot((u * u).astype(bf16), w2_ref[cs, :])
        if c == 0:
            acc_ref[...] = part
        else:
            acc_ref[...] += part
    o_ref[...] = x1 + mod_ref[5:6, :] * acc_ref[...]


def _out_mlp(a, x, mod, ng, w_out, w1, w2):
    s, d = x.shape
    tm = ROW_TILE
    wo, w1b, w2b = w_out.astype(bf16), w1.astype(bf16), w2.astype(bf16)
    row = lambda n: pl.BlockSpec((tm, n), lambda i: (i, 0))
    return pl.pallas_call(
        _out_mlp_kernel,
        out_shape=jax.ShapeDtypeStruct((s, d), f32),
        grid=(s // tm,),
        in_specs=[row(a.shape[1]), row(d), _resident(mod.shape), _resident(ng.shape),
                  _resident(wo.shape), _resident(w1b.shape), _resident(w2b.shape)],
        out_specs=row(d),
        scratch_shapes=[pltpu.VMEM((tm, d), f32)],
        compiler_params=_params("arbitrary"),
        name="out_proj_mlp",
    )(a, x, mod, ng, wo, w1b, w2b)


def kernel(x, c, positions, ada_w, ada_b, norm_mix_g, norm_mlp_g, gla_w_in, gla_w_gate_up, gla_b_gate,
           gla_onorm_g, gla_w_out, moba_w_in, moba_q_norm_g, moba_k_norm_g, moba_w_out, mlp_w1, mlp_w2):
    b, s, d = x.shape
    assert b == 1 and s % ROW_TILE == 0 and s % MOBA_BLOCK == 0
    depth = ada_w.shape[0]
    xs = x.reshape(s, d)
    mod = _ada_mod(c, ada_w, ada_b)
    moba_hd = moba_w_out.shape[1] // MOBA_HEADS
    cos_t, sin_t = _rope_tables(positions, moba_hd)
    for i in range(depth):
        j = i // 2
        ng = norm_mix_g[i].reshape(1, d)
        if i % 2 == 0:
            q, k, v, r, g = _gla_in(xs, mod[i], ng, gla_w_in[j], gla_w_gate_up[j], gla_b_gate[j])
            a = _gla_core(q, k, g, v, r, gla_onorm_g[j])
            w_out = gla_w_out[j]
        else:
            q, k, vt, kmean = _moba_in(xs, mod[i], ng, moba_w_in[j], moba_q_norm_g[j], moba_k_norm_g[j],
                                       cos_t, sin_t)
            a = _moba_attn(q, k, vt, kmean.reshape(kmean.shape[0], kmean.shape[2]),
                           moba_q_norm_g[j], moba_k_norm_g[j])
            w_out = moba_w_out[j]
        xs = _out_mlp(a, xs, mod[i], norm_mlp_g[i].reshape(1, d), w_out, mlp_w1[i], mlp_w2[i])
    return xs.reshape(b, s, d)
```

```python
import functools

import jax
import jax.numpy as jnp
from jax import lax
from jax.experimental import pallas as pl
from jax.experimental.pallas import tpu as pltpu

f32 = jnp.float32
bf16 = jnp.bfloat16

EPS = 1e-6
GLA_HEADS = 4
GLA_GATE_RANK = 16
GLA_GATE_NORM = 16.0
GLA_CHUNK = 64
MOBA_HEADS = 8
MOBA_BLOCK = 256
MOBA_TOPK = 3
MOBA_GROUP = 8
MOBA_QTILE = 512
LOG2E = 1.4426950408889634
MOBA_MASK_VALUE = -2.0 ** 30
MOBA_FAST_RANGE = 64.0
ROPE_THETA = 10000.0

LANES = 128
VMEM_LIMIT = 56 * 1024 * 1024

ROW_TILE = 512
GLA_STEP = 256
FF_CHUNK = 1024


def _params(*sem):
    return pltpu.CompilerParams(dimension_semantics=sem, vmem_limit_bytes=VMEM_LIMIT)


def _resident(shape):
    nd = len(shape)
    return pl.BlockSpec(shape, lambda *_: (0,) * nd, pipeline_mode=pl.Buffered(1))


def _dot(a, b):
    return jnp.dot(a, b, preferred_element_type=f32)


def _dot_nt(a, b):
    return lax.dot_general(a, b, (((1,), (1,)), ((), ())), preferred_element_type=f32)


def _split3(x):
    x1 = x.astype(bf16)
    e1 = x - x1.astype(f32)
    x2 = e1.astype(bf16)
    x3 = (e1 - x2.astype(f32)).astype(bf16)
    return x1, x2, x3


def _silu(x):
    return x / (1.0 + jnp.exp(-x))


def _norm_mod(x, g, scale, shift):
    ms = jnp.mean(x * x, axis=-1, keepdims=True)
    y = x * lax.rsqrt(ms + EPS)
    return (y * g) * (1.0 + scale) + shift


def _ada_kernel(c_ref, w_ref, b_ref, o_ref):
    ca = _silu(c_ref[...])
    o_ref[...] = jnp.sum(ca * w_ref[...], axis=0, keepdims=True) + b_ref[...]


def _ada_mod(c, ada_w, ada_b):
    depth, d, n = ada_w.shape
    tn = 1536
    out = pl.pallas_call(
        _ada_kernel,
        out_shape=jax.ShapeDtypeStruct((depth, 1, n), f32),
        grid=(depth, n // tn),
        in_specs=[
            pl.BlockSpec((d, 1), lambda l, j: (0, 0)),
            pl.BlockSpec((None, d, tn), lambda l, j: (l, 0, j)),
            pl.BlockSpec((None, 1, tn), lambda l, j: (l, 0, j)),
        ],
        out_specs=pl.BlockSpec((None, 1, tn), lambda l, j: (l, 0, j)),
        compiler_params=_params("arbitrary", "arbitrary"),
        name="ada_mod",
    )(c.reshape(d, 1), ada_w, ada_b.reshape(depth, 1, n))
    return out.reshape(depth, 6, d)


def _rope_table_kernel(pos_ref, freq_ref, sign_ref, cos_ref, sin_ref):
    ang = pos_ref[...].astype(f32) * freq_ref[...]
    cos_ref[...] = jnp.cos(ang)
    sin_ref[...] = jnp.sin(ang) * sign_ref[...]


def _rope_tables(positions, hd):
    s = positions.shape[-1]
    half = hd // 2
    inv_freq = ROPE_THETA ** (-jnp.arange(half, dtype=f32) / half)
    freq = jnp.concatenate([inv_freq, inv_freq]).reshape(1, hd)
    sign = jnp.concatenate([-jnp.ones((half,), f32), jnp.ones((half,), f32)]).reshape(1, hd)
    tm = ROW_TILE
    return pl.pallas_call(
        _rope_table_kernel,
        out_shape=(jax.ShapeDtypeStruct((s, hd), f32), jax.ShapeDtypeStruct((s, hd), f32)),
        grid=(s // tm,),
        in_specs=[
            pl.BlockSpec((tm, 1), lambda i: (i, 0)),
            pl.BlockSpec((1, hd), lambda i: (0, 0)),
            pl.BlockSpec((1, hd), lambda i: (0, 0)),
        ],
        out_specs=(pl.BlockSpec((tm, hd), lambda i: (i, 0)), pl.BlockSpec((tm, hd), lambda i: (i, 0))),
        compiler_params=_params("arbitrary"),
        name="rope_tables",
    )(positions.reshape(s, 1), freq, sign)


def _gla_in_kernel(x_ref, mod_ref, ng_ref, w_ref, wglr_ref, wup_ref, bg_ref,
                   q_ref, k_ref, v_ref, r_ref, g_ref, *, hdk, hdv, rank):
    h = _norm_mod(x_ref[...], ng_ref[...], mod_ref[1:2, :], mod_ref[0:1, :]).astype(bf16)
    glr3 = _dot(h, wglr_ref[...])
    hi = glr3.astype(bf16).astype(f32)
    lane = lax.broadcasted_iota(jnp.int32, glr3.shape, 1)
    use_lo = jnp.logical_and(lane >= rank, lane < 2 * rank)
    z = _dot(jnp.where(use_lo, glr3 - hi, hi).astype(bf16), wup_ref[...]) + bg_ref[...]
    log_sig = jnp.minimum(z, 0.0) - jnp.log1p(jnp.exp(-jnp.abs(z)))
    g_ref[...] = log_sig / GLA_GATE_NORM
    q_ref[...] = _dot(h, w_ref[:, 0:hdk])
    k_ref[...] = _dot(h, w_ref[:, hdk:2 * hdk])
    v_ref[...] = _dot(h, w_ref[:, 2 * hdk:2 * hdk + hdv])
    r_ref[...] = _dot(h, w_ref[:, 2 * hdk + hdv:2 * hdk + 2 * hdv])


def _gla_in(x, mod, ng, w_in, w_gate_up, b_gate):
    s, d = x.shape
    hdk = w_gate_up.shape[1]
    rank = w_gate_up.shape[0]
    hdv = (w_in.shape[1] - 2 * hdk - rank) // 2
    n_main = 2 * hdk + 2 * hdv
    assert 3 * rank <= LANES
    w_main = w_in[:, :n_main].astype(bf16)
    w_glr = jnp.pad(jnp.tile(w_in[:, n_main:], (1, 3)), ((0, 0), (0, LANES - 3 * rank))).astype(bf16)
    wup_hi = w_gate_up.astype(bf16)
    wup_lo = (w_gate_up - wup_hi.astype(f32)).astype(bf16)
    wup = jnp.pad(jnp.concatenate([wup_hi, wup_hi, wup_lo], axis=0), ((0, LANES - 3 * rank), (0, 0)))
    tm = ROW_TILE
    row = lambda n: pl.BlockSpec((tm, n), lambda i: (i, 0))
    return pl.pallas_call(
        functools.partial(_gla_in_kernel, hdk=hdk, hdv=hdv, rank=rank),
        out_shape=(jax.ShapeDtypeStruct((s, hdk), f32), jax.ShapeDtypeStruct((s, hdk), f32),
                   jax.ShapeDtypeStruct((s, hdv), f32), jax.ShapeDtypeStruct((s, hdv), f32),
                   jax.ShapeDtypeStruct((s, hdk), f32)),
        grid=(s // tm,),
        in_specs=[row(d), _resident(mod.shape), _resident(ng.shape), _resident(w_main.shape),
                  _resident(w_glr.shape), _resident(wup.shape), _resident((1, hdk))],
        out_specs=(row(hdk), row(hdk), row(hdv), row(hdv), row(hdk)),
        compiler_params=_params("arbitrary"),
        name="gla_in_proj",
    )(x, mod, ng, w_main, w_glr, wup, b_gate.reshape(1, hdk))


def _gla_core_kernel(q_ref, k_ref, g_ref, v_ref, r_ref, og_ref, o_ref, s_ref, *, heads):
    @pl.when(pl.program_id(0) == 0)
    def _():
        s_ref[...] = jnp.zeros_like(s_ref)

    t = q_ref.shape[0]
    dk = q_ref.shape[1] // heads
    dv = v_ref.shape[1] // heads
    lc = GLA_CHUNK
    nch = t // lc
    row = lax.broadcasted_iota(jnp.int32, (t, t), 0)
    col = lax.broadcasted_iota(jnp.int32, (t, t), 1)
    tri = jnp.logical_and(col <= row, (row // lc) == (col // lc))
    tri_b = jnp.where(tri, 1.0, 0.0).astype(bf16)
    lane_chunk = lax.broadcasted_iota(jnp.int32, (dk, t), 1) // lc
    row_chunk = lax.broadcasted_iota(jnp.int32, (t, dk), 0) // lc
    scale = dk ** -0.5

    g1, g2, g3 = _split3(g_ref[...])
    gc = _dot(tri_b, g1) + _dot(tri_b, g2) + _dot(tri_b, g3)
    g_last = jnp.concatenate(
        [jnp.broadcast_to(gc[c * lc + lc - 1:c * lc + lc, :], (lc, heads * dk)) for c in range(nch)], axis=0)
    k_all = k_ref[...]
    q_dec_all = q_ref[...] * scale * jnp.exp(gc)
    k_inv_all = (k_all * jnp.exp(-gc)).astype(bf16)
    k_tail_t_all = (k_all * jnp.exp(g_last - gc)).T
    gc_t_all = gc.T

    for h in range(heads):
        ks = slice(h * dk, (h + 1) * dk)
        vs = slice(h * dv, (h + 1) * dv)
        q_dec = q_dec_all[:, ks]
        k_tail_t = k_tail_t_all[ks, :]
        vh = v_ref[:, vs].astype(bf16)
        a = jnp.where(tri, _dot_nt(q_dec.astype(bf16), k_inv_all[:, ks]), 0.0).astype(bf16)
        kt_stack = jnp.concatenate(
            [jnp.where(lane_chunk == c, k_tail_t, 0.0).astype(bf16) for c in range(nch)], axis=0)
        u_all = _dot(kt_stack, vh)
        state = s_ref[h]
        states = []
        for c in range(nch):
            states.append(state.astype(bf16))
            decay = jnp.exp(gc_t_all[ks, c * lc + lc - 1:c * lc + lc])
            state = decay * state + u_all[c * dk:(c + 1) * dk, :]
        s_ref[h] = state
        q_blocks = [jnp.where(row_chunk == c, q_dec, 0.0).astype(bf16) for c in range(nch)]
        o = _dot(jnp.concatenate([a] + q_blocks, axis=1), jnp.concatenate([vh] + states, axis=0))
        ms = jnp.mean(o * o, axis=-1, keepdims=True)
        on = o * lax.rsqrt(ms + EPS) * og_ref[...]
        o_ref[:, vs] = (on * _silu(r_ref[:, vs])).astype(bf16)


def _gla_core(q, k, g, v, r, onorm_g):
    s, hdk = q.shape
    hdv = v.shape[1]
    heads = GLA_HEADS
    t = GLA_STEP
    row = lambda n: pl.BlockSpec((t, n), lambda i: (i, 0))
    return pl.pallas_call(
        functools.partial(_gla_core_kernel, heads=heads),
        out_shape=jax.ShapeDtypeStruct((s, hdv), bf16),
        grid=(s // t,),
        in_specs=[row(hdk), row(hdk), row(hdk), row(hdv), row(hdv), _resident((1, hdv // heads))],
        out_specs=row(hdv),
        scratch_shapes=[pltpu.VMEM((heads, hdk // heads, hdv // heads), f32)],
        compiler_params=_params("arbitrary"),
        name="gla_core",
    )(q, k, g, v, r, onorm_g.reshape(1, hdv // heads))


def _moba_in_kernel(x_ref, mod_ref, ng_ref, w_ref, qg_ref, kg_ref, cos_ref, sin_ref,
                    qt_ref, k_ref, vt_ref, kmean_ref, *, heads):
    h = _norm_mod(x_ref[...], ng_ref[...], mod_ref[1:2, :], mod_ref[0:1, :]).astype(bf16)
    tm = x_ref.shape[0]
    d_attn = qt_ref.shape[0]
    hd = d_attn // heads
    nb = pl.num_programs(0) * (tm // MOBA_BLOCK)
    cos = cos_ref[...]
    sin = sin_ref[...]

    def norm_rope(y, g):
        ms = jnp.mean(y * y, axis=-1, keepdims=True)
        yn = y * lax.rsqrt(ms + EPS) * g
        return yn * cos + pltpu.roll(yn, hd // 2, 1) * sin

    row_blk = pl.program_id(0) * (tm // MOBA_BLOCK) + lax.broadcasted_iota(jnp.int32, (tm, hd), 0) // MOBA_BLOCK
    col = lax.broadcasted_iota(jnp.int32, (tm, hd), 1)
    extra = jnp.where(jnp.logical_or(col == row_blk, col == nb), 1.0, 0.0).astype(bf16)

    for pair in range(heads // 2):
        cs = slice(2 * pair * hd, 2 * (pair + 1) * hd)
        yq = _dot(h, w_ref[:, cs])
        for u in range(2):
            hh = 2 * pair + u
            qt_ref[hh * hd:(hh + 1) * hd, :] = norm_rope(yq[:, u * hd:(u + 1) * hd], qg_ref[...]).T
    for pair in range(heads // 2):
        yk = _dot(h, w_ref[:, d_attn + 2 * pair * hd:d_attn + 2 * (pair + 1) * hd])
        for u in range(2):
            hh = 2 * pair + u
            hs = slice(hh * hd, (hh + 1) * hd)
            kr = norm_rope(yk[:, u * hd:(u + 1) * hd], kg_ref[...])
            k_ref[:, 2 * hh * hd:(2 * hh + 1) * hd] = kr.astype(bf16)
            k_ref[:, (2 * hh + 1) * hd:(2 * hh + 2) * hd] = extra
            for b in range(tm // MOBA_BLOCK):
                kmean_ref[b, :, hs] = jnp.mean(kr[b * MOBA_BLOCK:(b + 1) * MOBA_BLOCK, :], axis=0, keepdims=True)
    for pair in range(heads // 2):
        cs = slice(2 * pair * hd, 2 * (pair + 1) * hd)
        yv = _dot(h, w_ref[:, 2 * d_attn + 2 * pair * hd:2 * d_attn + 2 * (pair + 1) * hd])
        vt_ref[cs, :] = yv.T.astype(bf16)


def _moba_in(x, mod, ng, w_in, qg, kg, cos_t, sin_t):
    s, d = x.shape
    heads = MOBA_HEADS
    d_attn = w_in.shape[1] // 3
    hd = d_attn // heads
    tm = ROW_TILE
    nb = s // MOBA_BLOCK
    assert nb + 1 <= hd
    row = lambda n: pl.BlockSpec((tm, n), lambda i: (i, 0))
    col = lambda n: pl.BlockSpec((n, tm), lambda i: (0, i))
    w = w_in.astype(bf16)
    return pl.pallas_call(
        functools.partial(_moba_in_kernel, heads=heads),
        out_shape=(jax.ShapeDtypeStruct((d_attn, s), f32), jax.ShapeDtypeStruct((s, 2 * d_attn), bf16),
                   jax.ShapeDtypeStruct((d_attn, s), bf16), jax.ShapeDtypeStruct((nb, 1, d_attn), f32)),
        grid=(s // tm,),
        in_specs=[row(d), _resident(mod.shape), _resident(ng.shape), _resident(w.shape),
                  _resident((1, hd)), _resident((1, hd)), row(hd), row(hd)],
        out_specs=(col(d_attn), row(2 * d_attn), col(d_attn),
                   pl.BlockSpec((tm // MOBA_BLOCK, 1, d_attn), lambda i: (i, 0, 0))),
        compiler_params=_params("arbitrary"),
        name="moba_in_proj",
    )(x, mod, ng, w, qg.reshape(1, hd), kg.reshape(1, hd), cos_t, sin_t)


def _moba_gate_bias(kmean_ref, qt, own_blk, off_value):
    nb = kmean_ref.shape[0]
    tq = qt.shape[1]
    neg = jnp.float32(-jnp.inf)
    gate = jnp.dot(kmean_ref[...], qt, preferred_element_type=f32, precision=lax.Precision.HIGHEST)
    blk = lax.broadcasted_iota(jnp.int32, (nb, tq), 0)
    gate = jnp.where(blk < own_blk, gate, neg)
    bias = jnp.full((nb, tq), off_value, f32)
    for _ in range(MOBA_TOPK):
        top = jnp.max(gate, axis=0, keepdims=True)
        is_top = jnp.logical_and(gate == top, top > neg)
        first = jnp.min(jnp.where(is_top, blk, nb), axis=0, keepdims=True)
        pick = blk == first
        bias = jnp.where(pick, 0.0, bias)
        gate = jnp.where(pick, neg, gate)
    return bias


def _causal_own_scores(k_ref, qb, blk):
    bs = MOBA_BLOCK
    hd = qb.shape[0]
    st = _dot(k_ref[pl.ds(pl.multiple_of(blk * bs, bs), bs), 0:hd], qb)
    key_pos = lax.broadcasted_iota(jnp.int32, (bs, bs), 0)
    qry_pos = lax.broadcasted_iota(jnp.int32, (bs, bs), 1)
    return jnp.where(key_pos <= qry_pos, st, jnp.float32(-jnp.inf))


def _moba_attn_fast_kernel(qt_ref, k_ref, vt_ref, kmean_ref, o_ref, acc_ref, l_ref, qaug_ref):
    hd, tq = qt_ref.shape
    bs = MOBA_BLOCK
    nt = tq // bs
    gk = MOBA_GROUP * bs
    nb = kmean_ref.shape[0]
    first = pl.program_id(1) * nt
    qt = qt_ref[...]
    own_blk = first + lax.broadcasted_iota(jnp.int32, (1, tq), 1) // bs
    bias = _moba_gate_bias(kmean_ref, qt, own_blk, MOBA_MASK_VALUE)
    qb = (qt * (hd ** -0.5 * LOG2E)).astype(bf16)

    shifts = []
    for a in range(nt):
        ls = slice(a * bs, (a + 1) * bs)
        st = _causal_own_scores(k_ref, qb[:, ls], first + a)
        c = jnp.max(st, axis=0, keepdims=True).astype(bf16).astype(f32)
        p = jnp.exp2(st - c)
        l_ref[:, ls] = jnp.sum(p, axis=0, keepdims=True)
        own = pl.multiple_of((first + a) * bs, bs)
        acc_ref[:, ls] = _dot(vt_ref[:, pl.ds(own, bs)], p.astype(bf16))
        shifts.append(c)
    c = jnp.concatenate(shifts, axis=1)

    qaug_ref[0:hd, :] = qb
    qaug_ref[hd:hd + nb, :] = bias.astype(bf16)
    tail = lax.broadcasted_iota(jnp.int32, (hd - nb, tq), 0)
    qaug_ref[hd + nb:2 * hd, :] = jnp.where(tail == 0, jnp.broadcast_to(-c, tail.shape), 0.0).astype(bf16)

    def past_group(g, carry):
        off = pl.multiple_of(g * gk, gk)
        pg = jnp.exp2(_dot(k_ref[pl.ds(off, gk), :], qaug_ref[...]))
        l_ref[...] += jnp.sum(pg, axis=0, keepdims=True)
        acc_ref[...] += _dot(vt_ref[:, pl.ds(off, gk)], pg.astype(bf16))
        return carry

    n_past = first + nt - 1
    lax.fori_loop(0, (n_past + MOBA_GROUP - 1) // MOBA_GROUP, past_group, 0)
    o_ref[...] = (acc_ref[...] / l_ref[...]).T.astype(bf16)


def _moba_attn_general_kernel(qt_ref, k_ref, vt_ref, kmean_ref, o_ref, acc_ref, l_ref, m_ref, bias_ref):
    hd, tq = qt_ref.shape
    bs = MOBA_BLOCK
    nt = tq // bs
    for a in range(nt):
        ls = slice(a * bs, (a + 1) * bs)
        i = pl.program_id(1) * nt + a
        qt = qt_ref[:, ls]
        bias_ref[...] = _moba_gate_bias(kmean_ref, qt, i, -jnp.inf)
        qb = (qt * (hd ** -0.5 * LOG2E)).astype(bf16)
        st = _causal_own_scores(k_ref, qb, i)
        m0 = jnp.max(st, axis=0, keepdims=True)
        p = jnp.exp2(st - m0)
        m_ref[...] = m0
        l_ref[:, ls] = jnp.sum(p, axis=0, keepdims=True)
        own = pl.multiple_of(i * bs, bs)
        acc_ref[:, ls] = _dot(vt_ref[:, pl.ds(own, bs)], p.astype(bf16))

        def past_block(j, carry, ls=ls, qb=qb):
            off = pl.multiple_of(j * bs, bs)
            sj = _dot(k_ref[pl.ds(off, bs), 0:hd], qb) + bias_ref[pl.ds(j, 1), :]
            m_old = m_ref[...]
            m_new = jnp.maximum(m_old, jnp.max(sj, axis=0, keepdims=True))
            alpha = jnp.exp2(m_old - m_new)
            pj = jnp.exp2(sj - m_new)
            l_ref[:, ls] = alpha * l_ref[:, ls] + jnp.sum(pj, axis=0, keepdims=True)
            acc_ref[:, ls] = alpha * acc_ref[:, ls] + _dot(vt_ref[:, pl.ds(off, bs)], pj.astype(bf16))
            m_ref[...] = m_new
            return carry

        lax.fori_loop(0, i, past_block, 0)
        o_ref[ls, :] = (acc_ref[:, ls] / l_ref[:, ls]).T.astype(bf16)


def _moba_attn(qt, k, vt, kmean, qg, kg):
    d_attn, s = qt.shape
    heads = MOBA_HEADS
    hd = d_attn // heads
    tq = MOBA_QTILE
    bs = MOBA_BLOCK
    nb = s // bs
    assert nb % MOBA_GROUP == 0 and s % tq == 0 and tq % bs == 0

    def call(body, scratch):
        return pl.pallas_call(
            body,
            out_shape=jax.ShapeDtypeStruct((s, d_attn), bf16),
            grid=(heads, s // tq),
            in_specs=[
                pl.BlockSpec((hd, tq), lambda h, t: (h, t)),
                pl.BlockSpec((s, 2 * hd), lambda h, t: (0, h)),
                pl.BlockSpec((hd, s), lambda h, t: (h, 0)),
                pl.BlockSpec((nb, hd), lambda h, t: (0, h)),
            ],
            out_specs=pl.BlockSpec((tq, hd), lambda h, t: (t, h)),
            scratch_shapes=scratch,
            compiler_params=_params("arbitrary", "arbitrary"),
            name=body.__name__.strip("_"),
        )(qt, k, vt, kmean)

    stats = [pltpu.VMEM((hd, tq), f32), pltpu.VMEM((1, tq), f32)]
    fast = lambda: call(_moba_attn_fast_kernel, stats + [pltpu.VMEM((2 * hd, tq), bf16)])
    general = lambda: call(_moba_attn_general_kernel,
                           stats + [pltpu.VMEM((1, bs), f32), pltpu.VMEM((nb, bs), f32)])
    bound = 1.02 * hd ** 0.5 * LOG2E * jnp.max(jnp.abs(qg)) * jnp.max(jnp.abs(kg))
    return lax.cond(2.0 * bound <= MOBA_FAST_RANGE, fast, general)


def _out_mlp_kernel(a_ref, x_ref, mod_ref, ng_ref, wo_ref, w1_ref, w2_ref, o_ref, acc_ref):
    x1 = x_ref[...] + mod_ref[2:3, :] * _dot(a_ref[...], wo_ref[...])
    h = _norm_mod(x1, ng_ref[...], mod_ref[4:5, :], mod_ref[3:4, :]).astype(bf16)
    d_ff = w1_ref.shape[1]
    for c in range(d_ff // FF_CHUNK):
        cs = slice(c * FF_CHUNK, (c + 1) * FF_CHUNK)
        u = jnp.maximum(_dot(h, w1_ref[:, cs]), 0.0)
        part = _dot((u * u).astype(bf16), w2_ref[cs, :])
        if c == 0:
            acc_ref[...] = part
        else:
            acc_ref[...] += part
    o_ref[...] = x1 + mod_ref[5:6, :] * acc_ref[...]


def _out_mlp(a, x, mod, ng, w_out, w1, w2):
    s, d = x.shape
    tm = ROW_TILE
    wo, w1b, w2b = w_out.astype(bf16), w1.astype(bf16), w2.astype(bf16)
    row = lambda n: pl.BlockSpec((tm, n), lambda i: (i, 0))
    return pl.pallas_call(
        _out_mlp_kernel,
        out_shape=jax.ShapeDtypeStruct((s, d), f32),
        grid=(s // tm,),
        in_specs=[row(a.shape[1]), row(d), _resident(mod.shape), _resident(ng.shape),
                  _resident(wo.shape), _resident(w1b.shape), _resident(w2b.shape)],
        out_specs=row(d),
        scratch_shapes=[pltpu.VMEM((tm, d), f32)],
        compiler_params=_params("arbitrary"),
        name="out_proj_mlp",
    )(a, x, mod, ng, wo, w1b, w2b)


def kernel(x, c, positions, ada_w, ada_b, norm_mix_g, norm_mlp_g, gla_w_in, gla_w_gate_up, gla_b_gate,
           gla_onorm_g, gla_w_out, moba_w_in, moba_q_norm_g, moba_k_norm_g, moba_w_out, mlp_w1, mlp_w2):
    b, s, d = x.shape
    assert b == 1 and s % ROW_TILE == 0 and s % MOBA_BLOCK == 0
    depth = ada_w.shape[0]
    xs = x.reshape(s, d)
    mod = _ada_mod(c, ada_w, ada_b)
    moba_hd = moba_w_out.shape[1] // MOBA_HEADS
    cos_t, sin_t = _rope_tables(positions, moba_hd)
    for i in range(depth):
        j = i // 2
        ng = norm_mix_g[i].reshape(1, d)
        if i % 2 == 0:
            q, k, v, r, g = _gla_in(xs, mod[i], ng, gla_w_in[j], gla_w_gate_up[j], gla_b_gate[j])
            a = _gla_core(q, k, g, v, r, gla_onorm_g[j])
            w_out = gla_w_out[j]
        else:
            q, k, vt, kmean = _moba_in(xs, mod[i], ng, moba_w_in[j], moba_q_norm_g[j], moba_k_norm_g[j],
                                       cos_t, sin_t)
            a = _moba_attn(q, k, vt, kmean.reshape(kmean.shape[0], kmean.shape[2]),
                           moba_q_norm_g[j], moba_k_norm_g[j])
            w_out = moba_w_out[j]
        xs = _out_mlp(a, xs, mod[i], norm_mlp_g[i].reshape(1, d), w_out, mlp_w1[i], mlp_w2[i])
    return xs.reshape(b, s, d)
```

```python
import functools

import jax
import jax.numpy as jnp
from jax import lax
from jax.experimental import pallas as pl
from jax.experimental.pallas import tpu as pltpu

f32 = jnp.float32
bf16 = jnp.bfloat16

EPS = 1e-6
GLA_HEADS = 4
GLA_GATE_RANK = 16
GLA_GATE_NORM = 16.0
GLA_CHUNK = 64
MOBA_HEADS = 8
MOBA_BLOCK = 256
MOBA_TOPK = 3
MOBA_GROUP = 8
MOBA_QTILE = 1024
LOG2E = 1.4426950408889634
MOBA_MASK_VALUE = -2.0 ** 30
MOBA_FAST_RANGE = 32.0
ROPE_THETA = 10000.0

LANES = 128
VMEM_LIMIT = 56 * 1024 * 1024

ROW_TILE = 512
GLA_STEP = 256
FF_CHUNK = 1024


def _params(*sem):
    return pltpu.CompilerParams(dimension_semantics=sem, vmem_limit_bytes=VMEM_LIMIT)


def _resident(shape):
    nd = len(shape)
    return pl.BlockSpec(shape, lambda *_: (0,) * nd, pipeline_mode=pl.Buffered(1))


def _dot(a, b):
    return jnp.dot(a, b, preferred_element_type=f32)


def _dot_nt(a, b):
    return lax.dot_general(a, b, (((1,), (1,)), ((), ())), preferred_element_type=f32)


def _split3(x):
    x1 = x.astype(bf16)
    e1 = x - x1.astype(f32)
    x2 = e1.astype(bf16)
    x3 = (e1 - x2.astype(f32)).astype(bf16)
    return x1, x2, x3


def _silu(x):
    return x / (1.0 + jnp.exp(-x))


def _norm_mod(x, g, scale, shift):
    ms = jnp.mean(x * x, axis=-1, keepdims=True)
    y = x * lax.rsqrt(ms + EPS)
    return (y * g) * (1.0 + scale) + shift


def _ada_kernel(c_ref, w_ref, b_ref, o_ref):
    ca = _silu(c_ref[...])
    o_ref[...] = jnp.sum(ca * w_ref[...], axis=0, keepdims=True) + b_ref[...]


def _ada_mod(c, ada_w, ada_b):
    depth, d, n = ada_w.shape
    tn = 1536
    out = pl.pallas_call(
        _ada_kernel,
        out_shape=jax.ShapeDtypeStruct((depth, 1, n), f32),
        grid=(depth, n // tn),
        in_specs=[
            pl.BlockSpec((d, 1), lambda l, j: (0, 0)),
            pl.BlockSpec((None, d, tn), lambda l, j: (l, 0, j)),
            pl.BlockSpec((None, 1, tn), lambda l, j: (l, 0, j)),
        ],
        out_specs=pl.BlockSpec((None, 1, tn), lambda l, j: (l, 0, j)),
        compiler_params=_params("arbitrary", "arbitrary"),
        name="ada_mod",
    )(c.reshape(d, 1), ada_w, ada_b.reshape(depth, 1, n))
    return out.reshape(depth, 6, d)


def _rope_table_kernel(pos_ref, freq_ref, sign_ref, cos_ref, sin_ref):
    ang = pos_ref[...].astype(f32) * freq_ref[...]
    cos_ref[...] = jnp.cos(ang)
    sin_ref[...] = jnp.sin(ang) * sign_ref[...]


def _rope_tables(positions, hd):
    s = positions.shape[-1]
    half = hd // 2
    inv_freq = ROPE_THETA ** (-jnp.arange(half, dtype=f32) / half)
    freq = jnp.concatenate([inv_freq, inv_freq]).reshape(1, hd)
    sign = jnp.concatenate([-jnp.ones((half,), f32), jnp.ones((half,), f32)]).reshape(1, hd)
    tm = ROW_TILE
    return pl.pallas_call(
        _rope_table_kernel,
        out_shape=(jax.ShapeDtypeStruct((s, hd), f32), jax.ShapeDtypeStruct((s, hd), f32)),
        grid=(s // tm,),
        in_specs=[
            pl.BlockSpec((tm, 1), lambda i: (i, 0)),
            pl.BlockSpec((1, hd), lambda i: (0, 0)),
            pl.BlockSpec((1, hd), lambda i: (0, 0)),
        ],
        out_specs=(pl.BlockSpec((tm, hd), lambda i: (i, 0)), pl.BlockSpec((tm, hd), lambda i: (i, 0))),
        compiler_params=_params("arbitrary"),
        name="rope_tables",
    )(positions.reshape(s, 1), freq, sign)


def _gla_in_kernel(x_ref, mod_ref, ng_ref, w_ref, wglr_ref, wup_ref, bg_ref,
                   q_ref, k_ref, v_ref, r_ref, g_ref, *, hdk, hdv, rank):
    h = _norm_mod(x_ref[...], ng_ref[...], mod_ref[1:2, :], mod_ref[0:1, :]).astype(bf16)
    glr3 = _dot(h, wglr_ref[...])
    hi = glr3.astype(bf16).astype(f32)
    lane = lax.broadcasted_iota(jnp.int32, glr3.shape, 1)
    use_lo = jnp.logical_and(lane >= rank, lane < 2 * rank)
    z = _dot(jnp.where(use_lo, glr3 - hi, hi).astype(bf16), wup_ref[...]) + bg_ref[...]
    log_sig = jnp.minimum(z, 0.0) - jnp.log1p(jnp.exp(-jnp.abs(z)))
    g_ref[...] = log_sig / GLA_GATE_NORM
    q_ref[...] = _dot(h, w_ref[:, 0:hdk])
    k_ref[...] = _dot(h, w_ref[:, hdk:2 * hdk])
    v_ref[...] = _dot(h, w_ref[:, 2 * hdk:2 * hdk + hdv])
    r_ref[...] = _dot(h, w_ref[:, 2 * hdk + hdv:2 * hdk + 2 * hdv])


def _gla_in(x, mod, ng, w_in, w_gate_up, b_gate):
    s, d = x.shape
    hdk = w_gate_up.shape[1]
    rank = w_gate_up.shape[0]
    hdv = (w_in.shape[1] - 2 * hdk - rank) // 2
    n_main = 2 * hdk + 2 * hdv
    assert 3 * rank <= LANES
    w_main = w_in[:, :n_main].astype(bf16)
    w_glr = jnp.pad(jnp.tile(w_in[:, n_main:], (1, 3)), ((0, 0), (0, LANES - 3 * rank))).astype(bf16)
    wup_hi = w_gate_up.astype(bf16)
    wup_lo = (w_gate_up - wup_hi.astype(f32)).astype(bf16)
    wup = jnp.pad(jnp.concatenate([wup_hi, wup_hi, wup_lo], axis=0), ((0, LANES - 3 * rank), (0, 0)))
    tm = ROW_TILE
    row = lambda n: pl.BlockSpec((tm, n), lambda i: (i, 0))
    return pl.pallas_call(
        functools.partial(_gla_in_kernel, hdk=hdk, hdv=hdv, rank=rank),
        out_shape=(jax.ShapeDtypeStruct((s, hdk), f32), jax.ShapeDtypeStruct((s, hdk), f32),
                   jax.ShapeDtypeStruct((s, hdv), f32), jax.ShapeDtypeStruct((s, hdv), f32),
                   jax.ShapeDtypeStruct((s, hdk), f32)),
        grid=(s // tm,),
        in_specs=[row(d), _resident(mod.shape), _resident(ng.shape), _resident(w_main.shape),
                  _resident(w_glr.shape), _resident(wup.shape), _resident((1, hdk))],
        out_specs=(row(hdk), row(hdk), row(hdv), row(hdv), row(hdk)),
        compiler_params=_params("arbitrary"),
        name="gla_in_proj",
    )(x, mod, ng, w_main, w_glr, wup, b_gate.reshape(1, hdk))


def _gla_core_kernel(q_ref, k_ref, g_ref, v_ref, r_ref, og_ref, o_ref, s_ref, *, heads):
    @pl.when(pl.program_id(0) == 0)
    def _():
        s_ref[...] = jnp.zeros_like(s_ref)

    t = q_ref.shape[0]
    dk = q_ref.shape[1] // heads
    dv = v_ref.shape[1] // heads
    lc = GLA_CHUNK
    nch = t // lc
    row = lax.broadcasted_iota(jnp.int32, (t, t), 0)
    col = lax.broadcasted_iota(jnp.int32, (t, t), 1)
    tri = jnp.logical_and(col <= row, (row // lc) == (col // lc))
    tri_b = jnp.where(tri, 1.0, 0.0).astype(bf16)
    lane_chunk = lax.broadcasted_iota(jnp.int32, (dk, t), 1) // lc
    row_chunk = lax.broadcasted_iota(jnp.int32, (t, dk), 0) // lc
    scale = dk ** -0.5

    g1, g2, g3 = _split3(g_ref[...])
    gc = _dot(tri_b, g1) + _dot(tri_b, g2) + _dot(tri_b, g3)
    g_last = jnp.concatenate(
        [jnp.broadcast_to(gc[c * lc + lc - 1:c * lc + lc, :], (lc, heads * dk)) for c in range(nch)], axis=0)
    k_all = k_ref[...]
    q_dec_all = q_ref[...] * scale * jnp.exp(gc)
    k_inv_all = (k_all * jnp.exp(-gc)).astype(bf16)
    k_tail_t_all = (k_all * jnp.exp(g_last - gc)).T
    gc_t_all = gc.T

    for h in range(heads):
        ks = slice(h * dk, (h + 1) * dk)
        vs = slice(h * dv, (h + 1) * dv)
        q_dec = q_dec_all[:, ks]
        k_tail_t = k_tail_t_all[ks, :]
        vh = v_ref[:, vs].astype(bf16)
        a = jnp.where(tri, _dot_nt(q_dec.astype(bf16), k_inv_all[:, ks]), 0.0).astype(bf16)
        kt_stack = jnp.concatenate(
            [jnp.where(lane_chunk == c, k_tail_t, 0.0).astype(bf16) for c in range(nch)], axis=0)
        u_all = _dot(kt_stack, vh)
        state = s_ref[h]
        states = []
        for c in range(nch):
            states.append(state.astype(bf16))
            decay = jnp.exp(gc_t_all[ks, c * lc + lc - 1:c * lc + lc])
            state = decay * state + u_all[c * dk:(c + 1) * dk, :]
        s_ref[h] = state
        q_blocks = [jnp.where(row_chunk == c, q_dec, 0.0).astype(bf16) for c in range(nch)]
        o = _dot(jnp.concatenate([a] + q_blocks, axis=1), jnp.concatenate([vh] + states, axis=0))
        ms = jnp.mean(o * o, axis=-1, keepdims=True)
        on = o * lax.rsqrt(ms + EPS) * og_ref[...]
        o_ref[:, vs] = (on * _silu(r_ref[:, vs])).astype(bf16)


def _gla_core(q, k, g, v, r, onorm_g):
    s, hdk = q.shape
    hdv = v.shape[1]
    heads = GLA_HEADS
    t = GLA_STEP
    row = lambda n: pl.BlockSpec((t, n), lambda i: (i, 0))
    return pl.pallas_call(
        functools.partial(_gla_core_kernel, heads=heads),
        out_shape=jax.ShapeDtypeStruct((s, hdv), bf16),
        grid=(s // t,),
        in_specs=[row(hdk), row(hdk), row(hdk), row(hdv), row(hdv), _resident((1, hdv // heads))],
        out_specs=row(hdv),
        scratch_shapes=[pltpu.VMEM((heads, hdk // heads, hdv // heads), f32)],
        compiler_params=_params("arbitrary"),
        name="gla_core",
    )(q, k, g, v, r, onorm_g.reshape(1, hdv // heads))


def _moba_in_kernel(x_ref, mod_ref, ng_ref, w_ref, qg_ref, kg_ref, cos_ref, sin_ref,
                    qt_ref, k_ref, vt_ref, kmean_ref, *, heads):
    h = _norm_mod(x_ref[...], ng_ref[...], mod_ref[1:2, :], mod_ref[0:1, :]).astype(bf16)
    tm = x_ref.shape[0]
    d_attn = qt_ref.shape[0]
    hd = d_attn // heads
    cos = cos_ref[...]
    sin = sin_ref[...]

    def norm_rope(y, g):
        ms = jnp.mean(y * y, axis=-1, keepdims=True)
        yn = y * lax.rsqrt(ms + EPS) * g
        return yn * cos + pltpu.roll(yn, hd // 2, 1) * sin

    row_blk = pl.program_id(0) * (tm // MOBA_BLOCK) + lax.broadcasted_iota(jnp.int32, (tm, hd), 0) // MOBA_BLOCK
    col = lax.broadcasted_iota(jnp.int32, (tm, hd), 1)
    extra = jnp.where(col == row_blk, 1.0, 0.0).astype(bf16)

    for pair in range(heads // 2):
        cs = slice(2 * pair * hd, 2 * (pair + 1) * hd)
        yq = _dot(h, w_ref[:, cs])
        for u in range(2):
            hh = 2 * pair + u
            qt_ref[hh * hd:(hh + 1) * hd, :] = norm_rope(yq[:, u * hd:(u + 1) * hd], qg_ref[...]).T
    for pair in range(heads // 2):
        yk = _dot(h, w_ref[:, d_attn + 2 * pair * hd:d_attn + 2 * (pair + 1) * hd])
        for u in range(2):
            hh = 2 * pair + u
            hs = slice(hh * hd, (hh + 1) * hd)
            kr = norm_rope(yk[:, u * hd:(u + 1) * hd], kg_ref[...])
            k_ref[:, 2 * hh * hd:(2 * hh + 1) * hd] = kr.astype(bf16)
            k_ref[:, (2 * hh + 1) * hd:(2 * hh + 2) * hd] = extra
            for b in range(tm // MOBA_BLOCK):
                kmean_ref[b, :, hs] = jnp.mean(kr[b * MOBA_BLOCK:(b + 1) * MOBA_BLOCK, :], axis=0, keepdims=True)
    for pair in range(heads // 2):
        cs = slice(2 * pair * hd, 2 * (pair + 1) * hd)
        yv = _dot(h, w_ref[:, 2 * d_attn + 2 * pair * hd:2 * d_attn + 2 * (pair + 1) * hd])
        vt_ref[cs, :] = yv.T.astype(bf16)


def _moba_in(x, mod, ng, w_in, qg, kg, cos_t, sin_t):
    s, d = x.shape
    heads = MOBA_HEADS
    d_attn = w_in.shape[1] // 3
    hd = d_attn // heads
    tm = ROW_TILE
    nb = s // MOBA_BLOCK
    assert nb <= hd
    row = lambda n: pl.BlockSpec((tm, n), lambda i: (i, 0))
    col = lambda n: pl.BlockSpec((n, tm), lambda i: (0, i))
    w = w_in.astype(bf16)
    return pl.pallas_call(
        functools.partial(_moba_in_kernel, heads=heads),
        out_shape=(jax.ShapeDtypeStruct((d_attn, s), f32), jax.ShapeDtypeStruct((s, 2 * d_attn), bf16),
                   jax.ShapeDtypeStruct((d_attn, s), bf16), jax.ShapeDtypeStruct((nb, 1, d_attn), f32)),
        grid=(s // tm,),
        in_specs=[row(d), _resident(mod.shape), _resident(ng.shape), _resident(w.shape),
                  _resident((1, hd)), _resident((1, hd)), row(hd), row(hd)],
        out_specs=(col(d_attn), row(2 * d_attn), col(d_attn),
                   pl.BlockSpec((tm // MOBA_BLOCK, 1, d_attn), lambda i: (i, 0, 0))),
        compiler_params=_params("arbitrary"),
        name="moba_in_proj",
    )(x, mod, ng, w, qg.reshape(1, hd), kg.reshape(1, hd), cos_t, sin_t)


def _moba_gate_bias(kmean_ref, qt, own_blk, off_value):
    nb = kmean_ref.shape[0]
    tq = qt.shape[1]
    neg = jnp.float32(-jnp.inf)
    km = kmean_ref[...]
    km_hi = km.astype(bf16)
    km_lo = (km - km_hi.astype(f32)).astype(bf16)
    qt_hi = qt.astype(bf16)
    qt_lo = (qt - qt_hi.astype(f32)).astype(bf16)
    gate = _dot(km_hi, qt_hi) + _dot(km_lo, qt_hi) + _dot(km_hi, qt_lo)
    blk = lax.broadcasted_iota(jnp.int32, (nb, tq), 0)
    gate = jnp.where(blk < own_blk, gate, neg)
    bias = jnp.full((nb, tq), off_value, f32)
    for _ in range(MOBA_TOPK):
        top = jnp.max(gate, axis=0, keepdims=True)
        is_top = jnp.logical_and(gate == top, top > neg)
        first = jnp.min(jnp.where(is_top, blk, nb), axis=0, keepdims=True)
        pick = blk == first
        bias = jnp.where(pick, 0.0, bias)
        gate = jnp.where(pick, neg, gate)
    return bias


def _causal_own_scores(k_ref, qb, blk):
    bs = MOBA_BLOCK
    hd = qb.shape[0]
    st = _dot(k_ref[pl.ds(pl.multiple_of(blk * bs, bs), bs), 0:hd], qb)
    key_pos = lax.broadcasted_iota(jnp.int32, (bs, bs), 0)
    qry_pos = lax.broadcasted_iota(jnp.int32, (bs, bs), 1)
    return jnp.where(key_pos <= qry_pos, st, jnp.float32(-jnp.inf))


def _moba_attn_fast_kernel(qt_ref, k_ref, vt_ref, kmean_ref, o_ref, acc_ref, l_ref, qaug_ref):
    hd, tq = qt_ref.shape
    bs = MOBA_BLOCK
    nt = tq // bs
    nb = kmean_ref.shape[0]
    first = pl.program_id(1) * nt
    qt = qt_ref[...]
    own_blk = first + lax.broadcasted_iota(jnp.int32, (1, tq), 1) // bs
    bias = _moba_gate_bias(kmean_ref, qt, own_blk, MOBA_MASK_VALUE)
    blk = lax.broadcasted_iota(jnp.int32, (nb, tq), 0)
    bias = jnp.where(blk == own_blk, 0.0, bias)
    qaug_ref[0:hd, :] = (qt * (hd ** -0.5 * LOG2E)).astype(bf16)
    qaug_ref[hd:hd + nb, :] = bias.astype(bf16)
    qaug_ref[hd + nb:2 * hd, :] = jnp.zeros((hd - nb, tq), bf16)
    acc_ref[...] = jnp.zeros_like(acc_ref)
    l_ref[...] = jnp.zeros_like(l_ref)

    def group(g, nblk, causal):
        gk = nblk * bs
        off = pl.multiple_of(g * gk, gk)
        sg = _dot(k_ref[pl.ds(off, gk), :], qaug_ref[...])
        if causal:
            key_pos = off + lax.broadcasted_iota(jnp.int32, (gk, tq), 0)
            qry_pos = first * bs + lax.broadcasted_iota(jnp.int32, (gk, tq), 1)
            sg = jnp.where(key_pos <= qry_pos, sg, jnp.float32(-jnp.inf))
        pg = jnp.exp2(sg)
        l_ref[...] += jnp.sum(pg, axis=0, keepdims=True)
        acc_ref[...] += _dot(vt_ref[:, pl.ds(off, gk)], pg.astype(bf16))

    def past_group(g, carry):
        group(g, MOBA_GROUP, False)
        return carry

    def own_group(g, carry):
        group(g, nt, True)
        return carry

    n_past = first // MOBA_GROUP
    lax.fori_loop(0, n_past, past_group, 0)
    lax.fori_loop(n_past * (MOBA_GROUP // nt), first // nt + 1, own_group, 0)
    o_ref[...] = (acc_ref[...] / l_ref[...]).T.astype(bf16)


def _moba_attn_general_kernel(qt_ref, k_ref, vt_ref, kmean_ref, o_ref, acc_ref, l_ref, m_ref, bias_ref):
    hd, tq = qt_ref.shape
    bs = MOBA_BLOCK
    nt = tq // bs
    for a in range(nt):
        ls = slice(a * bs, (a + 1) * bs)
        i = pl.program_id(1) * nt + a
        qt = qt_ref[:, ls]
        bias_ref[...] = _moba_gate_bias(kmean_ref, qt, i, -jnp.inf)
        qb = (qt * (hd ** -0.5 * LOG2E)).astype(bf16)
        st = _causal_own_scores(k_ref, qb, i)
        m0 = jnp.max(st, axis=0, keepdims=True)
        p = jnp.exp2(st - m0)
        m_ref[...] = m0
        l_ref[:, ls] = jnp.sum(p, axis=0, keepdims=True)
        own = pl.multiple_of(i * bs, bs)
        acc_ref[:, ls] = _dot(vt_ref[:, pl.ds(own, bs)], p.astype(bf16))

        def past_block(j, carry, ls=ls, qb=qb):
            off = pl.multiple_of(j * bs, bs)
            sj = _dot(k_ref[pl.ds(off, bs), 0:hd], qb) + bias_ref[pl.ds(j, 1), :]
            m_old = m_ref[...]
            m_new = jnp.maximum(m_old, jnp.max(sj, axis=0, keepdims=True))
            alpha = jnp.exp2(m_old - m_new)
            pj = jnp.exp2(sj - m_new)
            l_ref[:, ls] = alpha * l_ref[:, ls] + jnp.sum(pj, axis=0, keepdims=True)
            acc_ref[:, ls] = alpha * acc_ref[:, ls] + _dot(vt_ref[:, pl.ds(off, bs)], pj.astype(bf16))
            m_ref[...] = m_new
            return carry

        lax.fori_loop(0, i, past_block, 0)
        o_ref[ls, :] = (acc_ref[:, ls] / l_ref[:, ls]).T.astype(bf16)


def _moba_attn(qt, k, vt, kmean, qg, kg):
    d_attn, s = qt.shape
    heads = MOBA_HEADS
    hd = d_attn // heads
    tq = MOBA_QTILE
    bs = MOBA_BLOCK
    nb = s // bs
    assert nb % MOBA_GROUP == 0 and s % tq == 0 and tq % bs == 0 and MOBA_GROUP % (tq // bs) == 0

    def call(body, scratch):
        return pl.pallas_call(
            body,
            out_shape=jax.ShapeDtypeStruct((s, d_attn), bf16),
            grid=(heads, s // tq),
            in_specs=[
                pl.BlockSpec((hd, tq), lambda h, t: (h, t)),
                pl.BlockSpec((s, 2 * hd), lambda h, t: (0, h)),
                pl.BlockSpec((hd, s), lambda h, t: (h, 0)),
                pl.BlockSpec((nb, hd), lambda h, t: (0, h)),
            ],
            out_specs=pl.BlockSpec((tq, hd), lambda h, t: (t, h)),
            scratch_shapes=scratch,
            compiler_params=_params("arbitrary", "arbitrary"),
            name=body.__name__.strip("_"),
        )(qt, k, vt, kmean)

    stats = [pltpu.VMEM((hd, tq), f32), pltpu.VMEM((1, tq), f32)]
    fast = lambda: call(_moba_attn_fast_kernel, stats + [pltpu.VMEM((2 * hd, tq), bf16)])
    general = lambda: call(_moba_attn_general_kernel,
                           stats + [pltpu.VMEM((1, bs), f32), pltpu.VMEM((nb, bs), f32)])
    bound = 1.02 * hd ** 0.5 * LOG2E * jnp.max(jnp.abs(qg)) * jnp.max(jnp.abs(kg))
    return lax.cond(bound <= MOBA_FAST_RANGE, fast, general)


def _out_mlp_kernel(a_ref, x_ref, mod_ref, ng_ref, wo_ref, w1_ref, w2_ref, o_ref, acc_ref):
    x1 = x_ref[...] + mod_ref[2:3, :] * _dot(a_ref[...], wo_ref[...])
    h = _norm_mod(x1, ng_ref[...], mod_ref[4:5, :], mod_ref[3:4, :]).astype(bf16)
    d_ff = w1_ref.shape[1]
    for c in range(d_ff // FF_CHUNK):
        cs = slice(c * FF_CHUNK, (c + 1) * FF_CHUNK)
        u = jnp.maximum(_dot(h, w1_ref[:, cs]), 0.0)
        part = _dot((u * u).astype(bf16), w2_ref[cs, :])
        if c == 0:
            acc_ref[...] = part
        else:
            acc_ref[...] += part
    o_ref[...] = x1 + mod_ref[5:6, :] * acc_ref[...]


def _out_mlp(a, x, mod, ng, w_out, w1, w2):
    s, d = x.shape
    tm = ROW_TILE
    wo, w1b, w2b = w_out.astype(bf16), w1.astype(bf16), w2.astype(bf16)
    row = lambda n: pl.BlockSpec((tm, n), lambda i: (i, 0))
    return pl.pallas_call(
        _out_mlp_kernel,
        out_shape=jax.ShapeDtypeStruct((s, d), f32),
        grid=(s // tm,),
        in_specs=[row(a.shape[1]), row(d), _resident(mod.shape), _resident(ng.shape),
                  _resident(wo.shape), _resident(w1b.shape), _resident(w2b.shape)],
        out_specs=row(d),
        scratch_shapes=[pltpu.VMEM((tm, d), f32)],
        compiler_params=_params("arbitrary"),
        name="out_proj_mlp",
    )(a, x, mod, ng, wo, w1b, w2b)


def kernel(x, c, positions, ada_w, ada_b, norm_mix_g, norm_mlp_g, gla_w_in, gla_w_gate_up, gla_b_gate,
           gla_onorm_g, gla_w_out, moba_w_in, moba_q_norm_g, moba_k_norm_g, moba_w_out, mlp_w1, mlp_w2):
    b, s, d = x.shape
    assert b == 1 and s % ROW_TILE == 0 and s % MOBA_BLOCK == 0
    depth = ada_w.shape[0]
    xs = x.reshape(s, d)
    mod = _ada_mod(c, ada_w, ada_b)
    moba_hd = moba_w_out.shape[1] // MOBA_HEADS
    cos_t, sin_t = _rope_tables(positions, moba_hd)
    for i in range(depth):
        j = i // 2
        ng = norm_mix_g[i].reshape(1, d)
        if i % 2 == 0:
            q, k, v, r, g = _gla_in(xs, mod[i], ng, gla_w_in[j], gla_w_gate_up[j], gla_b_gate[j])
            a = _gla_core(q, k, g, v, r, gla_onorm_g[j])
            w_out = gla_w_out[j]
        else:
            q, k, vt, kmean = _moba_in(xs, mod[i], ng, moba_w_in[j], moba_q_norm_g[j], moba_k_norm_g[j],
                                       cos_t, sin_t)
            a = _moba_attn(q, k, vt, kmean.reshape(kmean.shape[0], kmean.shape[2]),
                           moba_q_norm_g[j], moba_k_norm_g[j])
            w_out = moba_w_out[j]
        xs = _out_mlp(a, xs, mod[i], norm_mlp_g[i].reshape(1, d), w_out, mlp_w1[i], mlp_w2[i])
    return xs.reshape(b, s, d)
```

```python
import functools

import jax
import jax.numpy as jnp
from jax import lax
from jax.experimental import pallas as pl
from jax.experimental.pallas import tpu as pltpu

f32 = jnp.float32
bf16 = jnp.bfloat16

EPS = 1e-6
GLA_HEADS = 4
GLA_GATE_RANK = 16
GLA_GATE_NORM = 16.0
GLA_CHUNK = 64
MOBA_HEADS = 8
MOBA_BLOCK = 256
MOBA_TOPK = 3
MOBA_GROUP = 8
MOBA_QTILE = 1024
LOG2E = 1.4426950408889634
MOBA_MASK_VALUE = -2.0 ** 30
MOBA_FAST_RANGE = 32.0
ROPE_THETA = 10000.0

LANES = 128
VMEM_LIMIT = 56 * 1024 * 1024

ROW_TILE = 512
GLA_STEP = 256
FF_CHUNK = 1024


def _params(*sem):
    return pltpu.CompilerParams(dimension_semantics=sem, vmem_limit_bytes=VMEM_LIMIT)


def _resident(shape):
    nd = len(shape)
    return pl.BlockSpec(shape, lambda *_: (0,) * nd, pipeline_mode=pl.Buffered(1))


def _dot(a, b):
    return jnp.dot(a, b, preferred_element_type=f32)


def _dot_nt(a, b):
    return lax.dot_general(a, b, (((1,), (1,)), ((), ())), preferred_element_type=f32)


def _split2(x):
    hi = x.astype(bf16)
    return hi, (x - hi.astype(f32)).astype(bf16)


def _silu(x):
    return x / (1.0 + jnp.exp(-x))


def _norm_mod(x, g, scale, shift):
    ms = jnp.mean(x * x, axis=-1, keepdims=True)
    y = x * lax.rsqrt(ms + EPS)
    return (y * g) * (1.0 + scale) + shift


def _ada_kernel(c_ref, w_ref, b_ref, o_ref):
    ca = _silu(c_ref[...])
    o_ref[...] = jnp.sum(ca * w_ref[...], axis=0, keepdims=True) + b_ref[...]


def _ada_mod(c, ada_w, ada_b):
    depth, d, n = ada_w.shape
    tn = 1536
    out = pl.pallas_call(
        _ada_kernel,
        out_shape=jax.ShapeDtypeStruct((depth, 1, n), f32),
        grid=(depth, n // tn),
        in_specs=[
            pl.BlockSpec((d, 1), lambda l, j: (0, 0)),
            pl.BlockSpec((None, d, tn), lambda l, j: (l, 0, j)),
            pl.BlockSpec((None, 1, tn), lambda l, j: (l, 0, j)),
        ],
        out_specs=pl.BlockSpec((None, 1, tn), lambda l, j: (l, 0, j)),
        compiler_params=_params("arbitrary", "arbitrary"),
        name="ada_mod",
    )(c.reshape(d, 1), ada_w, ada_b.reshape(depth, 1, n))
    return out.reshape(depth, 6, d)


def _rope_table_kernel(pos_ref, freq_ref, sign_ref, cos_ref, sin_ref, cost_ref, sint_ref):
    ang = pos_ref[...].astype(f32) * freq_ref[...]
    cos = jnp.cos(ang)
    sin = jnp.sin(ang) * sign_ref[...]
    cos_ref[...] = cos
    sin_ref[...] = sin
    cost_ref[...] = cos.T
    sint_ref[...] = sin.T


def _rope_tables(positions, hd):
    s = positions.shape[-1]
    half = hd // 2
    inv_freq = ROPE_THETA ** (-jnp.arange(half, dtype=f32) / half)
    freq = jnp.concatenate([inv_freq, inv_freq]).reshape(1, hd)
    sign = jnp.concatenate([-jnp.ones((half,), f32), jnp.ones((half,), f32)]).reshape(1, hd)
    tm = ROW_TILE
    row = pl.BlockSpec((tm, hd), lambda i: (i, 0))
    col = pl.BlockSpec((hd, tm), lambda i: (0, i))
    return pl.pallas_call(
        _rope_table_kernel,
        out_shape=(jax.ShapeDtypeStruct((s, hd), f32), jax.ShapeDtypeStruct((s, hd), f32),
                   jax.ShapeDtypeStruct((hd, s), f32), jax.ShapeDtypeStruct((hd, s), f32)),
        grid=(s // tm,),
        in_specs=[
            pl.BlockSpec((tm, 1), lambda i: (i, 0)),
            pl.BlockSpec((1, hd), lambda i: (0, 0)),
            pl.BlockSpec((1, hd), lambda i: (0, 0)),
        ],
        out_specs=(row, row, col, col),
        compiler_params=_params("arbitrary"),
        name="rope_tables",
    )(positions.reshape(s, 1), freq, sign)


def _gla_in_kernel(x_ref, mod_ref, ng_ref, w_ref, wglr_ref, wup_ref, bg_ref,
                   q_ref, k_ref, v_ref, r_ref, g_ref, *, hdk, hdv, rank):
    h = _norm_mod(x_ref[...], ng_ref[...], mod_ref[1:2, :], mod_ref[0:1, :]).astype(bf16)
    glr3 = _dot(h, wglr_ref[...])
    hi = glr3.astype(bf16).astype(f32)
    lane = lax.broadcasted_iota(jnp.int32, glr3.shape, 1)
    use_lo = jnp.logical_and(lane >= rank, lane < 2 * rank)
    z = _dot(jnp.where(use_lo, glr3 - hi, hi).astype(bf16), wup_ref[...]) + bg_ref[...]
    log_sig = jnp.minimum(z, 0.0) - jnp.log1p(jnp.exp(-jnp.abs(z)))
    g_ref[...] = log_sig / GLA_GATE_NORM
    q_ref[...] = _dot(h, w_ref[:, 0:hdk])
    k_ref[...] = _dot(h, w_ref[:, hdk:2 * hdk])
    v_ref[...] = _dot(h, w_ref[:, 2 * hdk:2 * hdk + hdv])
    r_ref[...] = _dot(h, w_ref[:, 2 * hdk + hdv:2 * hdk + 2 * hdv])


def _gla_in(x, mod, ng, w_in, w_gate_up, b_gate):
    s, d = x.shape
    hdk = w_gate_up.shape[1]
    rank = w_gate_up.shape[0]
    hdv = (w_in.shape[1] - 2 * hdk - rank) // 2
    n_main = 2 * hdk + 2 * hdv
    assert 3 * rank <= LANES
    w_main = w_in.astype(bf16)
    w_glr = jnp.pad(jnp.tile(w_in[:, n_main:], (1, 3)), ((0, 0), (0, LANES - 3 * rank))).astype(bf16)
    wup_hi = w_gate_up.astype(bf16)
    wup_lo = (w_gate_up - wup_hi.astype(f32)).astype(bf16)
    wup = jnp.pad(jnp.concatenate([wup_hi, wup_hi, wup_lo], axis=0), ((0, LANES - 3 * rank), (0, 0)))
    tm = ROW_TILE
    row = lambda n: pl.BlockSpec((tm, n), lambda i: (i, 0))
    return pl.pallas_call(
        functools.partial(_gla_in_kernel, hdk=hdk, hdv=hdv, rank=rank),
        out_shape=(jax.ShapeDtypeStruct((s, hdk), f32), jax.ShapeDtypeStruct((s, hdk), f32),
                   jax.ShapeDtypeStruct((s, hdv), f32), jax.ShapeDtypeStruct((s, hdv), f32),
                   jax.ShapeDtypeStruct((s, hdk), f32)),
        grid=(s // tm,),
        in_specs=[row(d), _resident(mod.shape), _resident(ng.shape), _resident(w_main.shape),
                  _resident(w_glr.shape), _resident(wup.shape), _resident((1, hdk))],
        out_specs=(row(hdk), row(hdk), row(hdv), row(hdv), row(hdk)),
        compiler_params=_params("arbitrary"),
        name="gla_in_proj",
    )(x, mod, ng, w_main, w_glr, wup, b_gate.reshape(1, hdk))


def _gla_core_kernel(q_ref, k_ref, g_ref, v_ref, r_ref, og_ref, o_ref, s_ref, *, heads):
    @pl.when(pl.program_id(0) == 0)
    def _():
        s_ref[...] = jnp.zeros_like(s_ref)

    t = q_ref.shape[0]
    dk = q_ref.shape[1] // heads
    dv = v_ref.shape[1] // heads
    lc = GLA_CHUNK
    nch = t // lc
    row = lax.broadcasted_iota(jnp.int32, (t, t), 0)
    col = lax.broadcasted_iota(jnp.int32, (t, t), 1)
    tri = jnp.logical_and(col <= row, (row // lc) == (col // lc))
    tri_b = jnp.where(tri, 1.0, 0.0).astype(bf16)
    lane_chunk = lax.broadcasted_iota(jnp.int32, (dk, t), 1) // lc
    row_chunk = lax.broadcasted_iota(jnp.int32, (t, dk), 0) // lc
    scale = dk ** -0.5

    g_hi, g_lo = _split2(g_ref[...])
    gc = _dot(tri_b, g_hi) + _dot(tri_b, g_lo)
    g_last = jnp.concatenate(
        [jnp.broadcast_to(gc[c * lc + lc - 1:c * lc + lc, :], (lc, heads * dk)) for c in range(nch)], axis=0)
    k_all = k_ref[...]
    q_dec_all = q_ref[...] * scale * jnp.exp(gc)
    k_inv_all = (k_all * jnp.exp(-gc)).astype(bf16)
    k_tail_t_all = (k_all * jnp.exp(g_last - gc)).T
    gc_t_all = gc.T

    for h in range(heads):
        ks = slice(h * dk, (h + 1) * dk)
        vs = slice(h * dv, (h + 1) * dv)
        q_dec = q_dec_all[:, ks]
        k_tail_t = k_tail_t_all[ks, :]
        vh = v_ref[:, vs].astype(bf16)
        a = jnp.where(tri, _dot_nt(q_dec.astype(bf16), k_inv_all[:, ks]), 0.0).astype(bf16)
        kt_stack = jnp.concatenate(
            [jnp.where(lane_chunk == c, k_tail_t, 0.0).astype(bf16) for c in range(nch)], axis=0)
        u_all = _dot(kt_stack, vh)
        state = s_ref[h]
        states = []
        for c in range(nch):
            states.append(state.astype(bf16))
            decay = jnp.exp(gc_t_all[ks, c * lc + lc - 1:c * lc + lc])
            state = decay * state + u_all[c * dk:(c + 1) * dk, :]
        s_ref[h] = state
        q_blocks = [jnp.where(row_chunk == c, q_dec, 0.0).astype(bf16) for c in range(nch)]
        o = _dot(jnp.concatenate([a] + q_blocks, axis=1), jnp.concatenate([vh] + states, axis=0))
        ms = jnp.mean(o * o, axis=-1, keepdims=True)
        on = o * lax.rsqrt(ms + EPS) * og_ref[...]
        o_ref[:, vs] = (on * _silu(r_ref[:, vs])).astype(bf16)


def _gla_core(q, k, g, v, r, onorm_g):
    s, hdk = q.shape
    hdv = v.shape[1]
    heads = GLA_HEADS
    t = GLA_STEP
    row = lambda n: pl.BlockSpec((t, n), lambda i: (i, 0))
    return pl.pallas_call(
        functools.partial(_gla_core_kernel, heads=heads),
        out_shape=jax.ShapeDtypeStruct((s, hdv), bf16),
        grid=(s // t,),
        in_specs=[row(hdk), row(hdk), row(hdk), row(hdv), row(hdv), _resident((1, hdv // heads))],
        out_specs=row(hdv),
        scratch_shapes=[pltpu.VMEM((heads, hdk // heads, hdv // heads), f32)],
        compiler_params=_params("arbitrary"),
        name="gla_core",
    )(q, k, g, v, r, onorm_g.reshape(1, hdv // heads))


def _moba_in_kernel(x_ref, mod_ref, ng_ref, wqt_ref, wk_ref, wvt_ref, qg_ref, kg_ref, cos_ref, sin_ref,
                    cost_ref, sint_ref, qt_ref, k_ref, vt_ref, kmean_ref, *, heads):
    hf = _norm_mod(x_ref[...], ng_ref[...], mod_ref[1:2, :], mod_ref[0:1, :])
    h = hf.astype(bf16)
    ht = hf.T.astype(bf16)
    tm = x_ref.shape[0]
    d_attn = qt_ref.shape[0]
    hd = d_attn // heads
    half = hd // 2

    cos_t = cost_ref[0:half, :]
    sin_t = sint_ref[half:hd, :]
    yqt = _dot(wqt_ref[...], ht)
    for hh in range(heads):
        y = yqt[hh * hd:(hh + 1) * hd, :]
        yn = y * lax.rsqrt(jnp.mean(y * y, axis=0, keepdims=True) + EPS) * qg_ref[...]
        t1, t2 = yn[0:half, :], yn[half:hd, :]
        qt_ref[hh * hd:hh * hd + half, :] = t1 * cos_t - t2 * sin_t
        qt_ref[hh * hd + half:(hh + 1) * hd, :] = t2 * cos_t + t1 * sin_t

    row_blk = pl.program_id(0) * (tm // MOBA_BLOCK) + lax.broadcasted_iota(jnp.int32, (tm, hd), 0) // MOBA_BLOCK
    col = lax.broadcasted_iota(jnp.int32, (tm, hd), 1)
    extra = jnp.where(col == row_blk, 1.0, 0.0).astype(bf16)
    cos = cos_ref[...]
    sin = sin_ref[...]
    yk = _dot(h, wk_ref[...])
    for hh in range(heads):
        hs = slice(hh * hd, (hh + 1) * hd)
        y = yk[:, hs]
        yn = y * lax.rsqrt(jnp.mean(y * y, axis=-1, keepdims=True) + EPS) * kg_ref[...]
        kr = yn * cos + pltpu.roll(yn, half, 1) * sin
        k_ref[:, 2 * hh * hd:(2 * hh + 1) * hd] = kr.astype(bf16)
        k_ref[:, (2 * hh + 1) * hd:(2 * hh + 2) * hd] = extra
        for b in range(tm // MOBA_BLOCK):
            kmean_ref[b, :, hs] = jnp.mean(kr[b * MOBA_BLOCK:(b + 1) * MOBA_BLOCK, :], axis=0, keepdims=True)

    vt_ref[...] = _dot(wvt_ref[...], ht).astype(bf16)


def _moba_in(x, mod, ng, w_in, qg, kg, tables):
    s, d = x.shape
    heads = MOBA_HEADS
    d_attn = w_in.shape[1] // 3
    hd = d_attn // heads
    tm = ROW_TILE
    nb = s // MOBA_BLOCK
    assert nb <= hd
    cos_t, sin_t, cos_tt, sin_tt = tables
    row = lambda n: pl.BlockSpec((tm, n), lambda i: (i, 0))
    col = lambda n: pl.BlockSpec((n, tm), lambda i: (0, i))
    wqt = w_in[:, 0:d_attn].T.astype(bf16)
    wk = w_in[:, d_attn:2 * d_attn].astype(bf16)
    wvt = w_in[:, 2 * d_attn:].T.astype(bf16)
    return pl.pallas_call(
        functools.partial(_moba_in_kernel, heads=heads),
        out_shape=(jax.ShapeDtypeStruct((d_attn, s), f32), jax.ShapeDtypeStruct((s, 2 * d_attn), bf16),
                   jax.ShapeDtypeStruct((d_attn, s), bf16), jax.ShapeDtypeStruct((nb, 1, d_attn), f32)),
        grid=(s // tm,),
        in_specs=[row(d), _resident(mod.shape), _resident(ng.shape), _resident(wqt.shape), _resident(wk.shape),
                  _resident(wvt.shape), _resident((hd, 1)), _resident((1, hd)), row(hd), row(hd), col(hd), col(hd)],
        out_specs=(col(d_attn), row(2 * d_attn), col(d_attn),
                   pl.BlockSpec((tm // MOBA_BLOCK, 1, d_attn), lambda i: (i, 0, 0))),
        compiler_params=_params("arbitrary"),
        name="moba_in_proj",
    )(x, mod, ng, wqt, wk, wvt, qg.reshape(hd, 1), kg.reshape(1, hd), cos_t, sin_t, cos_tt, sin_tt)


def _moba_gate_bias(kmean_ref, qt, own_blk, off_value):
    nb = kmean_ref.shape[0]
    tq = qt.shape[1]
    neg = jnp.float32(-jnp.inf)
    km = kmean_ref[...]
    km_hi = km.astype(bf16)
    km_lo = (km - km_hi.astype(f32)).astype(bf16)
    qt_hi = qt.astype(bf16)
    qt_lo = (qt - qt_hi.astype(f32)).astype(bf16)
    gate = _dot(km_hi, qt_hi) + _dot(km_lo, qt_hi) + _dot(km_hi, qt_lo)
    blk = lax.broadcasted_iota(jnp.int32, (nb, tq), 0)
    gate = jnp.where(blk < own_blk, gate, neg)
    bias = jnp.full((nb, tq), off_value, f32)
    for _ in range(MOBA_TOPK):
        top = jnp.max(gate, axis=0, keepdims=True)
        is_top = jnp.logical_and(gate == top, top > neg)
        first = jnp.min(jnp.where(is_top, blk, nb), axis=0, keepdims=True)
        pick = blk == first
        bias = jnp.where(pick, 0.0, bias)
        gate = jnp.where(pick, neg, gate)
    return bias


def _causal_own_scores(k_ref, qb, blk):
    bs = MOBA_BLOCK
    hd = qb.shape[0]
    st = _dot(k_ref[pl.ds(pl.multiple_of(blk * bs, bs), bs), 0:hd], qb)
    key_pos = lax.broadcasted_iota(jnp.int32, (bs, bs), 0)
    qry_pos = lax.broadcasted_iota(jnp.int32, (bs, bs), 1)
    return jnp.where(key_pos <= qry_pos, st, jnp.float32(-jnp.inf))


def _moba_attn_fast_kernel(qt_ref, k_ref, vt_ref, kmean_ref, o_ref, acc_ref, l_ref, qaug_ref):
    hd, tq = qt_ref.shape
    bs = MOBA_BLOCK
    nt = tq // bs
    nb = kmean_ref.shape[0]
    first = pl.program_id(1) * nt
    qt = qt_ref[...]
    own_blk = first + lax.broadcasted_iota(jnp.int32, (1, tq), 1) // bs
    bias = _moba_gate_bias(kmean_ref, qt, own_blk, MOBA_MASK_VALUE)
    blk = lax.broadcasted_iota(jnp.int32, (nb, tq), 0)
    bias = jnp.where(blk == own_blk, 0.0, bias)
    qaug_ref[0:hd, :] = (qt * (hd ** -0.5 * LOG2E)).astype(bf16)
    qaug_ref[hd:hd + nb, :] = bias.astype(bf16)
    qaug_ref[hd + nb:2 * hd, :] = jnp.zeros((hd - nb, tq), bf16)
    acc_ref[...] = jnp.zeros_like(acc_ref)
    l_ref[...] = jnp.zeros_like(l_ref)

    def group(g, nblk, causal):
        gk = nblk * bs
        off = pl.multiple_of(g * gk, gk)
        sg = _dot(k_ref[pl.ds(off, gk), :], qaug_ref[...])
        if causal:
            key_pos = off + lax.broadcasted_iota(jnp.int32, (gk, tq), 0)
            qry_pos = first * bs + lax.broadcasted_iota(jnp.int32, (gk, tq), 1)
            sg = jnp.where(key_pos <= qry_pos, sg, jnp.float32(-jnp.inf))
        pg = jnp.exp2(sg)
        l_ref[...] += jnp.sum(pg, axis=0, keepdims=True)
        acc_ref[...] += _dot(vt_ref[:, pl.ds(off, gk)], pg.astype(bf16))

    def past_group(g, carry):
        group(g, MOBA_GROUP, False)
        return carry

    def own_group(g, carry):
        group(g, nt, True)
        return carry

    n_past = first // MOBA_GROUP
    lax.fori_loop(0, n_past, past_group, 0)
    lax.fori_loop(n_past * (MOBA_GROUP // nt), first // nt + 1, own_group, 0)
    o_ref[...] = (acc_ref[...] / l_ref[...]).T.astype(bf16)


def _moba_attn_general_kernel(qt_ref, k_ref, vt_ref, kmean_ref, o_ref, acc_ref, l_ref, m_ref, bias_ref):
    hd, tq = qt_ref.shape
    bs = MOBA_BLOCK
    nt = tq // bs
    for a in range(nt):
        ls = slice(a * bs, (a + 1) * bs)
        i = pl.program_id(1) * nt + a
        qt = qt_ref[:, ls]
        bias_ref[...] = _moba_gate_bias(kmean_ref, qt, i, -jnp.inf)
        qb = (qt * (hd ** -0.5 * LOG2E)).astype(bf16)
        st = _causal_own_scores(k_ref, qb, i)
        m0 = jnp.max(st, axis=0, keepdims=True)
        p = jnp.exp2(st - m0)
        m_ref[...] = m0
        l_ref[:, ls] = jnp.sum(p, axis=0, keepdims=True)
        own = pl.multiple_of(i * bs, bs)
        acc_ref[:, ls] = _dot(vt_ref[:, pl.ds(own, bs)], p.astype(bf16))

        def past_block(j, carry, ls=ls, qb=qb):
            off = pl.multiple_of(j * bs, bs)
            sj = _dot(k_ref[pl.ds(off, bs), 0:hd], qb) + bias_ref[pl.ds(j, 1), :]
            m_old = m_ref[...]
            m_new = jnp.maximum(m_old, jnp.max(sj, axis=0, keepdims=True))
            alpha = jnp.exp2(m_old - m_new)
            pj = jnp.exp2(sj - m_new)
            l_ref[:, ls] = alpha * l_ref[:, ls] + jnp.sum(pj, axis=0, keepdims=True)
            acc_ref[:, ls] = alpha * acc_ref[:, ls] + _dot(vt_ref[:, pl.ds(off, bs)], pj.astype(bf16))
            m_ref[...] = m_new
            return carry

        lax.fori_loop(0, i, past_block, 0)
        o_ref[ls, :] = (acc_ref[:, ls] / l_ref[:, ls]).T.astype(bf16)


def _moba_attn(qt, k, vt, kmean, qg, kg):
    d_attn, s = qt.shape
    heads = MOBA_HEADS
    hd = d_attn // heads
    tq = MOBA_QTILE
    bs = MOBA_BLOCK
    nb = s // bs
    assert nb % MOBA_GROUP == 0 and s % tq == 0 and tq % bs == 0 and MOBA_GROUP % (tq // bs) == 0

    def call(body, scratch):
        return pl.pallas_call(
            body,
            out_shape=jax.ShapeDtypeStruct((s, d_attn), bf16),
            grid=(heads, s // tq),
            in_specs=[
                pl.BlockSpec((hd, tq), lambda h, t: (h, t)),
                pl.BlockSpec((s, 2 * hd), lambda h, t: (0, h)),
                pl.BlockSpec((hd, s), lambda h, t: (h, 0)),
                pl.BlockSpec((nb, hd), lambda h, t: (0, h)),
            ],
            out_specs=pl.BlockSpec((tq, hd), lambda h, t: (t, h)),
            scratch_shapes=scratch,
            compiler_params=_params("arbitrary", "arbitrary"),
            name=body.__name__.strip("_"),
        )(qt, k, vt, kmean)

    stats = [pltpu.VMEM((hd, tq), f32), pltpu.VMEM((1, tq), f32)]
    fast = lambda: call(_moba_attn_fast_kernel, stats + [pltpu.VMEM((2 * hd, tq), bf16)])
    general = lambda: call(_moba_attn_general_kernel,
                           stats + [pltpu.VMEM((1, bs), f32), pltpu.VMEM((nb, bs), f32)])
    bound = 1.02 * hd ** 0.5 * LOG2E * jnp.max(jnp.abs(qg)) * jnp.max(jnp.abs(kg))
    return lax.cond(bound <= MOBA_FAST_RANGE, fast, general)


def _out_mlp_kernel(a_ref, x_ref, mod_ref, ng_ref, wo_ref, w1_ref, w2_ref, o_ref, acc_ref):
    x1 = x_ref[...] + mod_ref[2:3, :] * _dot(a_ref[...], wo_ref[...])
    h = _norm_mod(x1, ng_ref[...], mod_ref[4:5, :], mod_ref[3:4, :]).astype(bf16)
    d_ff = w1_ref.shape[1]
    for c in range(d_ff // FF_CHUNK):
        cs = slice(c * FF_CHUNK, (c + 1) * FF_CHUNK)
        u = jnp.maximum(_dot(h, w1_ref[:, cs]), 0.0)
        part = _dot((u * u).astype(bf16), w2_ref[cs, :])
        if c == 0:
            acc_ref[...] = part
        else:
            acc_ref[...] += part
    o_ref[...] = x1 + mod_ref[5:6, :] * acc_ref[...]


def _out_mlp(a, x, mod, ng, w_out, w1, w2):
    s, d = x.shape
    tm = ROW_TILE
    wo, w1b, w2b = w_out.astype(bf16), w1.astype(bf16), w2.astype(bf16)
    row = lambda n: pl.BlockSpec((tm, n), lambda i: (i, 0))
    return pl.pallas_call(
        _out_mlp_kernel,
        out_shape=jax.ShapeDtypeStruct((s, d), f32),
        grid=(s // tm,),
        in_specs=[row(a.shape[1]), row(d), _resident(mod.shape), _resident(ng.shape),
                  _resident(wo.shape), _resident(w1b.shape), _resident(w2b.shape)],
        out_specs=row(d),
        scratch_shapes=[pltpu.VMEM((tm, d), f32)],
        compiler_params=_params("arbitrary"),
        name="out_proj_mlp",
    )(a, x, mod, ng, wo, w1b, w2b)


def kernel(x, c, positions, ada_w, ada_b, norm_mix_g, norm_mlp_g, gla_w_in, gla_w_gate_up, gla_b_gate,
           gla_onorm_g, gla_w_out, moba_w_in, moba_q_norm_g, moba_k_norm_g, moba_w_out, mlp_w1, mlp_w2):
    b, s, d = x.shape
    assert b == 1 and s % ROW_TILE == 0 and s % MOBA_BLOCK == 0
    depth = ada_w.shape[0]
    xs = x.reshape(s, d)
    mod = _ada_mod(c, ada_w, ada_b)
    moba_hd = moba_w_out.shape[1] // MOBA_HEADS
    tables = _rope_tables(positions, moba_hd)
    for i in range(depth):
        j = i // 2
        ng = norm_mix_g[i].reshape(1, d)
        if i % 2 == 0:
            q, k, v, r, g = _gla_in(xs, mod[i], ng, gla_w_in[j], gla_w_gate_up[j], gla_b_gate[j])
            a = _gla_core(q, k, g, v, r, gla_onorm_g[j])
            w_out = gla_w_out[j]
        else:
            q, k, vt, kmean = _moba_in(xs, mod[i], ng, moba_w_in[j], moba_q_norm_g[j], moba_k_norm_g[j], tables)
            a = _moba_attn(q, k, vt, kmean.reshape(kmean.shape[0], kmean.shape[2]),
                           moba_q_norm_g[j], moba_k_norm_g[j])
            w_out = moba_w_out[j]
        xs = _out_mlp(a, xs, mod[i], norm_mlp_g[i].reshape(1, d), w_out, mlp_w1[i], mlp_w2[i])
    return xs.reshape(b, s, d)
```

```python
import functools

import jax
import jax.numpy as jnp
from jax import lax
from jax.experimental import pallas as pl
from jax.experimental.pallas import tpu as pltpu

f32 = jnp.float32
bf16 = jnp.bfloat16

EPS = 1e-6
GLA_HEADS = 4
GLA_GATE_RANK = 16
GLA_GATE_NORM = 16.0
GLA_CHUNK = 64
MOBA_HEADS = 8
MOBA_BLOCK = 256
MOBA_TOPK = 3
MOBA_GROUP = 8
MOBA_QTILE = 1024
LOG2E = 1.4426950408889634
MOBA_MASK_VALUE = -2.0 ** 30
MOBA_FAST_RANGE = 32.0
ROPE_THETA = 10000.0

LANES = 128
VMEM_LIMIT = 56 * 1024 * 1024

ROW_TILE = 512
GLA_STEP = 256
FF_CHUNK = 1024


def _params(*sem):
    return pltpu.CompilerParams(dimension_semantics=sem, vmem_limit_bytes=VMEM_LIMIT)


def _resident(shape):
    nd = len(shape)
    return pl.BlockSpec(shape, lambda *_: (0,) * nd, pipeline_mode=pl.Buffered(1))


def _dot(a, b):
    return jnp.dot(a, b, preferred_element_type=f32)


def _dot_nt(a, b):
    return lax.dot_general(a, b, (((1,), (1,)), ((), ())), preferred_element_type=f32)


def _split2(x):
    hi = x.astype(bf16)
    return hi, (x - hi.astype(f32)).astype(bf16)


def _silu(x):
    return x / (1.0 + jnp.exp(-x))


def _norm_mod(x, g, scale, shift):
    ms = jnp.mean(x * x, axis=-1, keepdims=True)
    y = x * lax.rsqrt(ms + EPS)
    return (y * g) * (1.0 + scale) + shift


def _ada_kernel(c_ref, w_ref, b_ref, o_ref):
    ca = _silu(c_ref[...])
    o_ref[...] = jnp.sum(ca * w_ref[...], axis=0, keepdims=True) + b_ref[...]


def _ada_mod(c, ada_w, ada_b):
    depth, d, n = ada_w.shape
    tn = 1536
    out = pl.pallas_call(
        _ada_kernel,
        out_shape=jax.ShapeDtypeStruct((depth, 1, n), f32),
        grid=(depth, n // tn),
        in_specs=[
            pl.BlockSpec((d, 1), lambda l, j: (0, 0)),
            pl.BlockSpec((None, d, tn), lambda l, j: (l, 0, j)),
            pl.BlockSpec((None, 1, tn), lambda l, j: (l, 0, j)),
        ],
        out_specs=pl.BlockSpec((None, 1, tn), lambda l, j: (l, 0, j)),
        compiler_params=_params("arbitrary", "arbitrary"),
        name="ada_mod",
    )(c.reshape(d, 1), ada_w, ada_b.reshape(depth, 1, n))
    return out.reshape(depth, 6, d)


def _rope_table_kernel(pos_ref, freq_ref, sign_ref, cos_ref, sin_ref, cost_ref, sint_ref):
    ang = pos_ref[...].astype(f32) * freq_ref[...]
    cos = jnp.cos(ang)
    sin = jnp.sin(ang) * sign_ref[...]
    cos_ref[...] = cos
    sin_ref[...] = sin
    cost_ref[...] = cos.T
    sint_ref[...] = sin.T


def _rope_tables(positions, hd):
    s = positions.shape[-1]
    half = hd // 2
    inv_freq = ROPE_THETA ** (-jnp.arange(half, dtype=f32) / half)
    freq = jnp.concatenate([inv_freq, inv_freq]).reshape(1, hd)
    sign = jnp.concatenate([-jnp.ones((half,), f32), jnp.ones((half,), f32)]).reshape(1, hd)
    tm = ROW_TILE
    row = pl.BlockSpec((tm, hd), lambda i: (i, 0))
    col = pl.BlockSpec((hd, tm), lambda i: (0, i))
    return pl.pallas_call(
        _rope_table_kernel,
        out_shape=(jax.ShapeDtypeStruct((s, hd), f32), jax.ShapeDtypeStruct((s, hd), f32),
                   jax.ShapeDtypeStruct((hd, s), f32), jax.ShapeDtypeStruct((hd, s), f32)),
        grid=(s // tm,),
        in_specs=[
            pl.BlockSpec((tm, 1), lambda i: (i, 0)),
            pl.BlockSpec((1, hd), lambda i: (0, 0)),
            pl.BlockSpec((1, hd), lambda i: (0, 0)),
        ],
        out_specs=(row, row, col, col),
        compiler_params=_params("arbitrary"),
        name="rope_tables",
    )(positions.reshape(s, 1), freq, sign)


def _gla_in_kernel(x_ref, mod_ref, ng_ref, w_ref, wglr_ref, wup_ref, bg_ref,
                   q_ref, k_ref, v_ref, r_ref, g_ref, *, hdk, hdv, rank):
    h = _norm_mod(x_ref[...], ng_ref[...], mod_ref[1:2, :], mod_ref[0:1, :]).astype(bf16)
    glr3 = _dot(h, wglr_ref[...])
    hi = glr3.astype(bf16).astype(f32)
    lane = lax.broadcasted_iota(jnp.int32, glr3.shape, 1)
    use_lo = jnp.logical_and(lane >= rank, lane < 2 * rank)
    z = _dot(jnp.where(use_lo, glr3 - hi, hi).astype(bf16), wup_ref[...]) + bg_ref[...]
    log_sig = jnp.minimum(z, 0.0) - jnp.log(1.0 + jnp.exp(-jnp.abs(z)))
    g_ref[...] = log_sig / GLA_GATE_NORM
    q_ref[...] = _dot(h, w_ref[:, 0:hdk].astype(bf16))
    k_ref[...] = _dot(h, w_ref[:, hdk:2 * hdk].astype(bf16))
    v_ref[...] = _dot(h, w_ref[:, 2 * hdk:2 * hdk + hdv].astype(bf16))
    r_ref[...] = _dot(h, w_ref[:, 2 * hdk + hdv:2 * hdk + 2 * hdv].astype(bf16))


def _gla_in(x, mod, ng, w_in, w_gate_up, b_gate):
    s, d = x.shape
    hdk = w_gate_up.shape[1]
    rank = w_gate_up.shape[0]
    hdv = (w_in.shape[1] - 2 * hdk - rank) // 2
    n_main = 2 * hdk + 2 * hdv
    assert 3 * rank <= LANES
    w_main = w_in
    w_glr = jnp.pad(jnp.tile(w_in[:, n_main:], (1, 3)), ((0, 0), (0, LANES - 3 * rank))).astype(bf16)
    wup_hi = w_gate_up.astype(bf16)
    wup_lo = (w_gate_up - wup_hi.astype(f32)).astype(bf16)
    wup = jnp.pad(jnp.concatenate([wup_hi, wup_hi, wup_lo], axis=0), ((0, LANES - 3 * rank), (0, 0)))
    tm = ROW_TILE
    row = lambda n: pl.BlockSpec((tm, n), lambda i: (i, 0))
    return pl.pallas_call(
        functools.partial(_gla_in_kernel, hdk=hdk, hdv=hdv, rank=rank),
        out_shape=(jax.ShapeDtypeStruct((s, hdk), f32), jax.ShapeDtypeStruct((s, hdk), f32),
                   jax.ShapeDtypeStruct((s, hdv), f32), jax.ShapeDtypeStruct((s, hdv), f32),
                   jax.ShapeDtypeStruct((s, hdk), f32)),
        grid=(s // tm,),
        in_specs=[row(d), _resident(mod.shape), _resident(ng.shape), _resident(w_main.shape),
                  _resident(w_glr.shape), _resident(wup.shape), _resident((1, hdk))],
        out_specs=(row(hdk), row(hdk), row(hdv), row(hdv), row(hdk)),
        compiler_params=_params("arbitrary"),
        name="gla_in_proj",
    )(x, mod, ng, w_main, w_glr, wup, b_gate.reshape(1, hdk))


def _gla_core_kernel(q_ref, k_ref, g_ref, v_ref, r_ref, og_ref, o_ref, s_ref, *, heads):
    @pl.when(pl.program_id(0) == 0)
    def _():
        s_ref[...] = jnp.zeros_like(s_ref)

    t = q_ref.shape[0]
    dk = q_ref.shape[1] // heads
    dv = v_ref.shape[1] // heads
    lc = GLA_CHUNK
    nch = t // lc
    row = lax.broadcasted_iota(jnp.int32, (t, t), 0)
    col = lax.broadcasted_iota(jnp.int32, (t, t), 1)
    tri = jnp.logical_and(col <= row, (row // lc) == (col // lc))
    tri_b = jnp.where(tri, 1.0, 0.0).astype(bf16)
    lane_chunk = lax.broadcasted_iota(jnp.int32, (dk, t), 1) // lc
    row_chunk = lax.broadcasted_iota(jnp.int32, (t, dk), 0) // lc
    scale = dk ** -0.5

    g_hi, g_lo = _split2(g_ref[...])
    gc = _dot(tri_b, g_hi) + _dot(tri_b, g_lo)
    g_last = jnp.concatenate(
        [jnp.broadcast_to(gc[c * lc + lc - 1:c * lc + lc, :], (lc, heads * dk)) for c in range(nch)], axis=0)
    k_all = k_ref[...]
    q_dec_all = q_ref[...] * scale * jnp.exp(gc)
    k_inv_all = (k_all * jnp.exp(-gc)).astype(bf16)
    k_tail_t_all = (k_all * jnp.exp(g_last - gc)).T
    gc_t_all = gc.T

    for h in range(heads):
        ks = slice(h * dk, (h + 1) * dk)
        vs = slice(h * dv, (h + 1) * dv)
        q_dec = q_dec_all[:, ks]
        k_tail_t = k_tail_t_all[ks, :]
        vh = v_ref[:, vs].astype(bf16)
        a = jnp.where(tri, _dot_nt(q_dec.astype(bf16), k_inv_all[:, ks]), 0.0).astype(bf16)
        kt_stack = jnp.concatenate(
            [jnp.where(lane_chunk == c, k_tail_t, 0.0).astype(bf16) for c in range(nch)], axis=0)
        u_all = _dot(kt_stack, vh)
        state = s_ref[h]
        states = []
        for c in range(nch):
            states.append(state.astype(bf16))
            decay = jnp.exp(gc_t_all[ks, c * lc + lc - 1:c * lc + lc])
            state = decay * state + u_all[c * dk:(c + 1) * dk, :]
        s_ref[h] = state
        q_blocks = [jnp.where(row_chunk == c, q_dec, 0.0).astype(bf16) for c in range(nch)]
        o = _dot(jnp.concatenate([a] + q_blocks, axis=1), jnp.concatenate([vh] + states, axis=0))
        ms = jnp.mean(o * o, axis=-1, keepdims=True)
        on = o * lax.rsqrt(ms + EPS) * og_ref[...]
        o_ref[:, vs] = (on * _silu(r_ref[:, vs])).astype(bf16)


def _gla_core(q, k, g, v, r, onorm_g):
    s, hdk = q.shape
    hdv = v.shape[1]
    heads = GLA_HEADS
    t = GLA_STEP
    row = lambda n: pl.BlockSpec((t, n), lambda i: (i, 0))
    return pl.pallas_call(
        functools.partial(_gla_core_kernel, heads=heads),
        out_shape=jax.ShapeDtypeStruct((s, hdv), bf16),
        grid=(s // t,),
        in_specs=[row(hdk), row(hdk), row(hdk), row(hdv), row(hdv), _resident((1, hdv // heads))],
        out_specs=row(hdv),
        scratch_shapes=[pltpu.VMEM((heads, hdk // heads, hdv // heads), f32)],
        compiler_params=_params("arbitrary"),
        name="gla_core",
    )(q, k, g, v, r, onorm_g.reshape(1, hdv // heads))


def _moba_in_kernel(x_ref, mod_ref, ng_ref, wqt_ref, wk_ref, wvt_ref, qg_ref, kg_ref, cos_ref, sin_ref,
                    cost_ref, sint_ref, qt_ref, k_ref, vt_ref, kmean_ref, *, heads):
    hf = _norm_mod(x_ref[...], ng_ref[...], mod_ref[1:2, :], mod_ref[0:1, :])
    h = hf.astype(bf16)
    ht = hf.T.astype(bf16)
    tm = x_ref.shape[0]
    d_attn = qt_ref.shape[0]
    hd = d_attn // heads
    half = hd // 2

    cos_t = cost_ref[0:half, :]
    sin_t = sint_ref[half:hd, :]
    yqt = _dot(wqt_ref[...], ht)
    for hh in range(heads):
        y = yqt[hh * hd:(hh + 1) * hd, :]
        yn = y * lax.rsqrt(jnp.mean(y * y, axis=0, keepdims=True) + EPS) * qg_ref[...]
        t1, t2 = yn[0:half, :], yn[half:hd, :]
        qt_ref[hh * hd:hh * hd + half, :] = t1 * cos_t - t2 * sin_t
        qt_ref[hh * hd + half:(hh + 1) * hd, :] = t2 * cos_t + t1 * sin_t

    row_blk = pl.program_id(0) * (tm // MOBA_BLOCK) + lax.broadcasted_iota(jnp.int32, (tm, hd), 0) // MOBA_BLOCK
    col = lax.broadcasted_iota(jnp.int32, (tm, hd), 1)
    extra = jnp.where(col == row_blk, 1.0, 0.0).astype(bf16)
    cos = cos_ref[...]
    sin = sin_ref[...]
    yk = _dot(h, wk_ref[...])
    for hh in range(heads):
        hs = slice(hh * hd, (hh + 1) * hd)
        y = yk[:, hs]
        yn = y * lax.rsqrt(jnp.mean(y * y, axis=-1, keepdims=True) + EPS) * kg_ref[...]
        kr = yn * cos + pltpu.roll(yn, half, 1) * sin
        k_ref[:, 2 * hh * hd:(2 * hh + 1) * hd] = kr.astype(bf16)
        k_ref[:, (2 * hh + 1) * hd:(2 * hh + 2) * hd] = extra
        for b in range(tm // MOBA_BLOCK):
            kmean_ref[b, :, hs] = jnp.mean(kr[b * MOBA_BLOCK:(b + 1) * MOBA_BLOCK, :], axis=0, keepdims=True)

    vt_ref[...] = _dot(wvt_ref[...], ht).astype(bf16)


def _moba_in(x, mod, ng, w_in, qg, kg, tables):
    s, d = x.shape
    heads = MOBA_HEADS
    d_attn = w_in.shape[1] // 3
    hd = d_attn // heads
    tm = ROW_TILE
    nb = s // MOBA_BLOCK
    assert nb <= hd
    cos_t, sin_t, cos_tt, sin_tt = tables
    row = lambda n: pl.BlockSpec((tm, n), lambda i: (i, 0))
    col = lambda n: pl.BlockSpec((n, tm), lambda i: (0, i))
    wqt = w_in[:, 0:d_attn].T.astype(bf16)
    wk = w_in[:, d_attn:2 * d_attn].astype(bf16)
    wvt = w_in[:, 2 * d_attn:].T.astype(bf16)
    return pl.pallas_call(
        functools.partial(_moba_in_kernel, heads=heads),
        out_shape=(jax.ShapeDtypeStruct((d_attn, s), f32), jax.ShapeDtypeStruct((s, 2 * d_attn), bf16),
                   jax.ShapeDtypeStruct((d_attn, s), bf16), jax.ShapeDtypeStruct((nb, 1, d_attn), f32)),
        grid=(s // tm,),
        in_specs=[row(d), _resident(mod.shape), _resident(ng.shape), _resident(wqt.shape), _resident(wk.shape),
                  _resident(wvt.shape), _resident((hd, 1)), _resident((1, hd)), row(hd), row(hd), col(hd), col(hd)],
        out_specs=(col(d_attn), row(2 * d_attn), col(d_attn),
                   pl.BlockSpec((tm // MOBA_BLOCK, 1, d_attn), lambda i: (i, 0, 0))),
        compiler_params=_params("arbitrary"),
        name="moba_in_proj",
    )(x, mod, ng, wqt, wk, wvt, qg.reshape(hd, 1), kg.reshape(1, hd), cos_t, sin_t, cos_tt, sin_tt)


def _moba_gate_bias(kmean_ref, qt, own_blk, off_value):
    nb = kmean_ref.shape[0]
    tq = qt.shape[1]
    neg = jnp.float32(-jnp.inf)
    km = kmean_ref[...]
    km_hi = km.astype(bf16)
    km_lo = (km - km_hi.astype(f32)).astype(bf16)
    qt_hi = qt.astype(bf16)
    qt_lo = (qt - qt_hi.astype(f32)).astype(bf16)
    gate = _dot(km_hi, qt_hi) + _dot(km_lo, qt_hi) + _dot(km_hi, qt_lo)
    blk = lax.broadcasted_iota(jnp.int32, (nb, tq), 0)
    gate = jnp.where(blk < own_blk, gate, neg)
    bias = jnp.full((nb, tq), off_value, f32)
    for _ in range(MOBA_TOPK):
        top = jnp.max(gate, axis=0, keepdims=True)
        is_top = jnp.logical_and(gate == top, top > neg)
        first = jnp.min(jnp.where(is_top, blk, nb), axis=0, keepdims=True)
        pick = blk == first
        bias = jnp.where(pick, 0.0, bias)
        gate = jnp.where(pick, neg, gate)
    return bias


def _causal_own_scores(k_ref, qb, blk):
    bs = MOBA_BLOCK
    hd = qb.shape[0]
    st = _dot(k_ref[pl.ds(pl.multiple_of(blk * bs, bs), bs), 0:hd], qb)
    key_pos = lax.broadcasted_iota(jnp.int32, (bs, bs), 0)
    qry_pos = lax.broadcasted_iota(jnp.int32, (bs, bs), 1)
    return jnp.where(key_pos <= qry_pos, st, jnp.float32(-jnp.inf))


def _moba_attn_fast_kernel(qt_ref, k_ref, vt_ref, kmean_ref, o_ref, acc_ref, l_ref, qaug_ref):
    hd, tq = qt_ref.shape
    bs = MOBA_BLOCK
    nt = tq // bs
    nb = kmean_ref.shape[0]
    first = pl.program_id(1) * nt
    qt = qt_ref[...]
    own_blk = first + lax.broadcasted_iota(jnp.int32, (1, tq), 1) // bs
    bias = _moba_gate_bias(kmean_ref, qt, own_blk, MOBA_MASK_VALUE)
    blk = lax.broadcasted_iota(jnp.int32, (nb, tq), 0)
    bias = jnp.where(blk == own_blk, 0.0, bias)
    qaug_ref[0:hd, :] = (qt * (hd ** -0.5 * LOG2E)).astype(bf16)
    qaug_ref[hd:hd + nb, :] = bias.astype(bf16)
    qaug_ref[hd + nb:2 * hd, :] = jnp.zeros((hd - nb, tq), bf16)
    acc_ref[...] = jnp.zeros_like(acc_ref)
    l_ref[...] = jnp.zeros_like(l_ref)

    def group(g, nblk, causal):
        gk = nblk * bs
        off = pl.multiple_of(g * gk, gk)
        sg = _dot(k_ref[pl.ds(off, gk), :], qaug_ref[...])
        if causal:
            key_pos = off + lax.broadcasted_iota(jnp.int32, (gk, tq), 0)
            qry_pos = first * bs + lax.broadcasted_iota(jnp.int32, (gk, tq), 1)
            sg = jnp.where(key_pos <= qry_pos, sg, jnp.float32(-jnp.inf))
        pg = jnp.exp2(sg)
        l_ref[...] += jnp.sum(pg, axis=0, keepdims=True)
        acc_ref[...] += _dot(vt_ref[:, pl.ds(off, gk)], pg.astype(bf16))

    def past_group(g, carry):
        group(g, MOBA_GROUP, False)
        return carry

    def own_group(g, carry):
        group(g, nt, True)
        return carry

    n_past = first // MOBA_GROUP
    lax.fori_loop(0, n_past, past_group, 0)
    lax.fori_loop(n_past * (MOBA_GROUP // nt), first // nt + 1, own_group, 0)
    o_ref[...] = (acc_ref[...] / l_ref[...]).T.astype(bf16)


def _moba_attn_general_kernel(qt_ref, k_ref, vt_ref, kmean_ref, o_ref, acc_ref, l_ref, m_ref, bias_ref):
    hd, tq = qt_ref.shape
    bs = MOBA_BLOCK
    nt = tq // bs
    for a in range(nt):
        ls = slice(a * bs, (a + 1) * bs)
        i = pl.program_id(1) * nt + a
        qt = qt_ref[:, ls]
        bias_ref[...] = _moba_gate_bias(kmean_ref, qt, i, -jnp.inf)
        qb = (qt * (hd ** -0.5 * LOG2E)).astype(bf16)
        st = _causal_own_scores(k_ref, qb, i)
        m0 = jnp.max(st, axis=0, keepdims=True)
        p = jnp.exp2(st - m0)
        m_ref[...] = m0
        l_ref[:, ls] = jnp.sum(p, axis=0, keepdims=True)
        own = pl.multiple_of(i * bs, bs)
        acc_ref[:, ls] = _dot(vt_ref[:, pl.ds(own, bs)], p.astype(bf16))

        def past_block(j, carry, ls=ls, qb=qb):
            off = pl.multiple_of(j * bs, bs)
            sj = _dot(k_ref[pl.ds(off, bs), 0:hd], qb) + bias_ref[pl.ds(j, 1), :]
            m_old = m_ref[...]
            m_new = jnp.maximum(m_old, jnp.max(sj, axis=0, keepdims=True))
            alpha = jnp.exp2(m_old - m_new)
            pj = jnp.exp2(sj - m_new)
            l_ref[:, ls] = alpha * l_ref[:, ls] + jnp.sum(pj, axis=0, keepdims=True)
            acc_ref[:, ls] = alpha * acc_ref[:, ls] + _dot(vt_ref[:, pl.ds(off, bs)], pj.astype(bf16))
            m_ref[...] = m_new
            return carry

        lax.fori_loop(0, i, past_block, 0)
        o_ref[ls, :] = (acc_ref[:, ls] / l_ref[:, ls]).T.astype(bf16)


def _moba_attn(qt, k, vt, kmean, qg, kg):
    d_attn, s = qt.shape
    heads = MOBA_HEADS
    hd = d_attn // heads
    tq = MOBA_QTILE
    bs = MOBA_BLOCK
    nb = s // bs
    assert nb % MOBA_GROUP == 0 and s % tq == 0 and tq % bs == 0 and MOBA_GROUP % (tq // bs) == 0

    def call(body, scratch):
        return pl.pallas_call(
            body,
            out_shape=jax.ShapeDtypeStruct((s, d_attn), bf16),
            grid=(heads, s // tq),
            in_specs=[
                pl.BlockSpec((hd, tq), lambda h, t: (h, t)),
                pl.BlockSpec((s, 2 * hd), lambda h, t: (0, h)),
                pl.BlockSpec((hd, s), lambda h, t: (h, 0)),
                pl.BlockSpec((nb, hd), lambda h, t: (0, h)),
            ],
            out_specs=pl.BlockSpec((tq, hd), lambda h, t: (t, h)),
            scratch_shapes=scratch,
            compiler_params=_params("arbitrary", "arbitrary"),
            name=body.__name__.strip("_"),
        )(qt, k, vt, kmean)

    stats = [pltpu.VMEM((hd, tq), f32), pltpu.VMEM((1, tq), f32)]
    fast = lambda: call(_moba_attn_fast_kernel, stats + [pltpu.VMEM((2 * hd, tq), bf16)])
    general = lambda: call(_moba_attn_general_kernel,
                           stats + [pltpu.VMEM((1, bs), f32), pltpu.VMEM((nb, bs), f32)])
    bound = 1.02 * hd ** 0.5 * LOG2E * jnp.max(jnp.abs(qg)) * jnp.max(jnp.abs(kg))
    return lax.cond(bound <= MOBA_FAST_RANGE, fast, general)


def _out_mlp_kernel(a_ref, x_ref, mod_ref, ng_ref, wo_ref, w1_ref, w2_ref, o_ref, acc_ref):
    x1 = x_ref[...] + mod_ref[2:3, :] * _dot(a_ref[...], wo_ref[...].astype(bf16))
    h = _norm_mod(x1, ng_ref[...], mod_ref[4:5, :], mod_ref[3:4, :]).astype(bf16)
    d_ff = w1_ref.shape[1]
    for c in range(d_ff // FF_CHUNK):
        cs = slice(c * FF_CHUNK, (c + 1) * FF_CHUNK)
        u = jnp.maximum(_dot(h, w1_ref[:, cs].astype(bf16)), 0.0)
        part = _dot((u * u).astype(bf16), w2_ref[cs, :].astype(bf16))
        if c == 0:
            acc_ref[...] = part
        else:
            acc_ref[...] += part
    o_ref[...] = x1 + mod_ref[5:6, :] * acc_ref[...]


def _out_mlp(a, x, mod, ng, w_out, w1, w2):
    s, d = x.shape
    tm = ROW_TILE
    row = lambda n: pl.BlockSpec((tm, n), lambda i: (i, 0))
    return pl.pallas_call(
        _out_mlp_kernel,
        out_shape=jax.ShapeDtypeStruct((s, d), f32),
        grid=(s // tm,),
        in_specs=[row(a.shape[1]), row(d), _resident(mod.shape), _resident(ng.shape),
                  _resident(w_out.shape), _resident(w1.shape), _resident(w2.shape)],
        out_specs=row(d),
        scratch_shapes=[pltpu.VMEM((tm, d), f32)],
        compiler_params=_params("arbitrary"),
        name="out_proj_mlp",
    )(a, x, mod, ng, w_out, w1, w2)


def kernel(x, c, positions, ada_w, ada_b, norm_mix_g, norm_mlp_g, gla_w_in, gla_w_gate_up, gla_b_gate,
           gla_onorm_g, gla_w_out, moba_w_in, moba_q_norm_g, moba_k_norm_g, moba_w_out, mlp_w1, mlp_w2):
    b, s, d = x.shape
    assert b == 1 and s % ROW_TILE == 0 and s % MOBA_BLOCK == 0
    depth = ada_w.shape[0]
    xs = x.reshape(s, d)
    mod = _ada_mod(c, ada_w, ada_b)
    moba_hd = moba_w_out.shape[1] // MOBA_HEADS
    tables = _rope_tables(positions, moba_hd)
    for i in range(depth):
        j = i // 2
        ng = norm_mix_g[i].reshape(1, d)
        if i % 2 == 0:
            q, k, v, r, g = _gla_in(xs, mod[i], ng, gla_w_in[j], gla_w_gate_up[j], gla_b_gate[j])
            a = _gla_core(q, k, g, v, r, gla_onorm_g[j])
            w_out = gla_w_out[j]
        else:
            q, k, vt, kmean = _moba_in(xs, mod[i], ng, moba_w_in[j], moba_q_norm_g[j], moba_k_norm_g[j], tables)
            a = _moba_attn(q, k, vt, kmean.reshape(kmean.shape[0], kmean.shape[2]),
                           moba_q_norm_g[j], moba_k_norm_g[j])
            w_out = moba_w_out[j]
        xs = _out_mlp(a, xs, mod[i], norm_mlp_g[i].reshape(1, d), w_out, mlp_w1[i], mlp_w2[i])
    return xs.reshape(b, s, d)
```

```python
import functools

import jax
import jax.numpy as jnp
from jax import lax
from jax.experimental import pallas as pl
from jax.experimental.pallas import tpu as pltpu

f32 = jnp.float32
bf16 = jnp.bfloat16

EPS = 1e-6
GLA_HEADS = 4
GLA_GATE_RANK = 16
GLA_GATE_NORM = 16.0
GLA_CHUNK = 64
MOBA_HEADS = 8
MOBA_BLOCK = 256
MOBA_TOPK = 3
MOBA_GROUP = 8
MOBA_QTILE = 1024
LOG2E = 1.4426950408889634
MOBA_MASK_VALUE = -2.0 ** 30
MOBA_FAST_RANGE = 32.0
ROPE_THETA = 10000.0

LANES = 128
VMEM_LIMIT = 56 * 1024 * 1024

ROW_TILE = 512
GLA_STEP = 256
FF_CHUNK = 1024


def _params(*sem):
    return pltpu.CompilerParams(dimension_semantics=sem, vmem_limit_bytes=VMEM_LIMIT)


def _resident(shape):
    nd = len(shape)
    return pl.BlockSpec(shape, lambda *_: (0,) * nd, pipeline_mode=pl.Buffered(1))


def _resident_layer(stacked_shape, layer):
    nd = len(stacked_shape) - 1
    return pl.BlockSpec((None,) + tuple(stacked_shape[1:]), lambda *_: (layer,) + (0,) * nd,
                        pipeline_mode=pl.Buffered(1))


def _dot(a, b):
    return jnp.dot(a, b, preferred_element_type=f32)


def _dot_nt(a, b):
    return lax.dot_general(a, b, (((1,), (1,)), ((), ())), preferred_element_type=f32)


def _split2(x):
    hi = x.astype(bf16)
    return hi, (x - hi.astype(f32)).astype(bf16)


def _silu(x):
    return x / (1.0 + jnp.exp(-x))


def _norm_mod(x, g, scale, shift):
    ms = jnp.mean(x * x, axis=-1, keepdims=True)
    y = x * lax.rsqrt(ms + EPS)
    return (y * g) * (1.0 + scale) + shift


def _ada_kernel(c_ref, w_ref, b_ref, o_ref):
    ca = _silu(c_ref[...])
    o_ref[...] = jnp.sum(ca * w_ref[...], axis=0, keepdims=True) + b_ref[...]


def _ada_mod(c, ada_w, ada_b):
    depth, d, n = ada_w.shape
    tn = 1536
    out = pl.pallas_call(
        _ada_kernel,
        out_shape=jax.ShapeDtypeStruct((depth, 1, n), f32),
        grid=(depth, n // tn),
        in_specs=[
            pl.BlockSpec((d, 1), lambda l, j: (0, 0)),
            pl.BlockSpec((None, d, tn), lambda l, j: (l, 0, j)),
            pl.BlockSpec((None, 1, tn), lambda l, j: (l, 0, j)),
        ],
        out_specs=pl.BlockSpec((None, 1, tn), lambda l, j: (l, 0, j)),
        compiler_params=_params("arbitrary", "arbitrary"),
        name="ada_mod",
    )(c.reshape(d, 1), ada_w, ada_b.reshape(depth, 1, n))
    return out.reshape(depth, 6, d)


def _rope_table_kernel(pos_ref, freq_ref, sign_ref, cos_ref, sin_ref, cost_ref, sint_ref):
    ang = pos_ref[...].astype(f32) * freq_ref[...]
    cos = jnp.cos(ang)
    sin = jnp.sin(ang) * sign_ref[...]
    cos_ref[...] = cos
    sin_ref[...] = sin
    cost_ref[...] = cos.T
    sint_ref[...] = sin.T


def _rope_tables(positions, hd):
    s = positions.shape[-1]
    half = hd // 2
    inv_freq = ROPE_THETA ** (-jnp.arange(half, dtype=f32) / half)
    freq = jnp.concatenate([inv_freq, inv_freq]).reshape(1, hd)
    sign = jnp.concatenate([-jnp.ones((half,), f32), jnp.ones((half,), f32)]).reshape(1, hd)
    tm = ROW_TILE
    row = pl.BlockSpec((tm, hd), lambda i: (i, 0))
    col = pl.BlockSpec((hd, tm), lambda i: (0, i))
    return pl.pallas_call(
        _rope_table_kernel,
        out_shape=(jax.ShapeDtypeStruct((s, hd), f32), jax.ShapeDtypeStruct((s, hd), f32),
                   jax.ShapeDtypeStruct((hd, s), f32), jax.ShapeDtypeStruct((hd, s), f32)),
        grid=(s // tm,),
        in_specs=[
            pl.BlockSpec((tm, 1), lambda i: (i, 0)),
            pl.BlockSpec((1, hd), lambda i: (0, 0)),
            pl.BlockSpec((1, hd), lambda i: (0, 0)),
        ],
        out_specs=(row, row, col, col),
        compiler_params=_params("arbitrary"),
        name="rope_tables",
    )(positions.reshape(s, 1), freq, sign)


def _gla_in_kernel(x_ref, mod_ref, ng_ref, w_ref, wglr_ref, wup_ref, bg_ref,
                   q_ref, k_ref, v_ref, r_ref, g_ref, *, hdk, hdv, rank):
    h = _norm_mod(x_ref[...], ng_ref[...], mod_ref[1:2, :], mod_ref[0:1, :]).astype(bf16)
    glr3 = _dot(h, wglr_ref[...])
    hi = glr3.astype(bf16).astype(f32)
    lane = lax.broadcasted_iota(jnp.int32, glr3.shape, 1)
    use_lo = jnp.logical_and(lane >= rank, lane < 2 * rank)
    z = _dot(jnp.where(use_lo, glr3 - hi, hi).astype(bf16), wup_ref[...]) + bg_ref[...]
    log_sig = jnp.minimum(z, 0.0) - jnp.log(1.0 + jnp.exp(-jnp.abs(z)))
    g_ref[...] = log_sig / GLA_GATE_NORM
    q_ref[...] = _dot(h, w_ref[:, 0:hdk].astype(bf16))
    k_ref[...] = _dot(h, w_ref[:, hdk:2 * hdk].astype(bf16))
    v_ref[...] = _dot(h, w_ref[:, 2 * hdk:2 * hdk + hdv].astype(bf16))
    r_ref[...] = _dot(h, w_ref[:, 2 * hdk + hdv:2 * hdk + 2 * hdv].astype(bf16))


def _gla_in(x, mod, ng, w_in_all, layer, w_gate_up, b_gate):
    s, d = x.shape
    w_in = w_in_all[layer]
    hdk = w_gate_up.shape[1]
    rank = w_gate_up.shape[0]
    hdv = (w_in.shape[1] - 2 * hdk - rank) // 2
    n_main = 2 * hdk + 2 * hdv
    assert 3 * rank <= LANES
    w_glr = jnp.pad(jnp.tile(w_in[:, n_main:], (1, 3)), ((0, 0), (0, LANES - 3 * rank))).astype(bf16)
    wup_hi = w_gate_up.astype(bf16)
    wup_lo = (w_gate_up - wup_hi.astype(f32)).astype(bf16)
    wup = jnp.pad(jnp.concatenate([wup_hi, wup_hi, wup_lo], axis=0), ((0, LANES - 3 * rank), (0, 0)))
    tm = ROW_TILE
    row = lambda n: pl.BlockSpec((tm, n), lambda i: (i, 0))
    return pl.pallas_call(
        functools.partial(_gla_in_kernel, hdk=hdk, hdv=hdv, rank=rank),
        out_shape=(jax.ShapeDtypeStruct((s, hdk), f32), jax.ShapeDtypeStruct((s, hdk), f32),
                   jax.ShapeDtypeStruct((s, hdv), f32), jax.ShapeDtypeStruct((s, hdv), f32),
                   jax.ShapeDtypeStruct((s, hdk), f32)),
        grid=(s // tm,),
        in_specs=[row(d), _resident(mod.shape), _resident(ng.shape), _resident_layer(w_in_all.shape, layer),
                  _resident(w_glr.shape), _resident(wup.shape), _resident((1, hdk))],
        out_specs=(row(hdk), row(hdk), row(hdv), row(hdv), row(hdk)),
        compiler_params=_params("arbitrary"),
        name="gla_in_proj",
    )(x, mod, ng, w_in_all, w_glr, wup, b_gate.reshape(1, hdk))


def _gla_core_kernel(q_ref, k_ref, g_ref, v_ref, r_ref, og_ref, o_ref, s_ref, *, heads):
    @pl.when(pl.program_id(0) == 0)
    def _():
        s_ref[...] = jnp.zeros_like(s_ref)

    t = q_ref.shape[0]
    dk = q_ref.shape[1] // heads
    dv = v_ref.shape[1] // heads
    lc = GLA_CHUNK
    nch = t // lc
    row = lax.broadcasted_iota(jnp.int32, (t, t), 0)
    col = lax.broadcasted_iota(jnp.int32, (t, t), 1)
    tri = jnp.logical_and(col <= row, (row // lc) == (col // lc))
    tri_b = jnp.where(tri, 1.0, 0.0).astype(bf16)
    lane_chunk = lax.broadcasted_iota(jnp.int32, (dk, t), 1) // lc
    row_chunk = lax.broadcasted_iota(jnp.int32, (t, dk), 0) // lc
    scale = dk ** -0.5

    g_hi, g_lo = _split2(g_ref[...])
    gc = _dot(tri_b, g_hi) + _dot(tri_b, g_lo)
    g_last = jnp.concatenate(
        [jnp.broadcast_to(gc[c * lc + lc - 1:c * lc + lc, :], (lc, heads * dk)) for c in range(nch)], axis=0)
    k_all = k_ref[...]
    q_dec_all = q_ref[...] * scale * jnp.exp(gc)
    k_inv_all = (k_all * jnp.exp(-gc)).astype(bf16)
    k_tail_t_all = (k_all * jnp.exp(g_last - gc)).T
    gc_t_all = gc.T

    for h in range(heads):
        ks = slice(h * dk, (h + 1) * dk)
        vs = slice(h * dv, (h + 1) * dv)
        q_dec = q_dec_all[:, ks]
        k_tail_t = k_tail_t_all[ks, :]
        vh = v_ref[:, vs].astype(bf16)
        a = jnp.where(tri, _dot_nt(q_dec.astype(bf16), k_inv_all[:, ks]), 0.0).astype(bf16)
        kt_stack = jnp.concatenate(
            [jnp.where(lane_chunk == c, k_tail_t, 0.0).astype(bf16) for c in range(nch)], axis=0)
        u_all = _dot(kt_stack, vh)
        state = s_ref[h]
        states = []
        for c in range(nch):
            states.append(state.astype(bf16))
            decay = jnp.exp(gc_t_all[ks, c * lc + lc - 1:c * lc + lc])
            state = decay * state + u_all[c * dk:(c + 1) * dk, :]
        s_ref[h] = state
        q_blocks = [jnp.where(row_chunk == c, q_dec, 0.0).astype(bf16) for c in range(nch)]
        o = _dot(jnp.concatenate([a] + q_blocks, axis=1), jnp.concatenate([vh] + states, axis=0))
        ms = jnp.mean(o * o, axis=-1, keepdims=True)
        on = o * lax.rsqrt(ms + EPS) * og_ref[...]
        o_ref[:, vs] = (on * _silu(r_ref[:, vs])).astype(bf16)


def _gla_core(q, k, g, v, r, onorm_g):
    s, hdk = q.shape
    hdv = v.shape[1]
    heads = GLA_HEADS
    t = GLA_STEP
    row = lambda n: pl.BlockSpec((t, n), lambda i: (i, 0))
    return pl.pallas_call(
        functools.partial(_gla_core_kernel, heads=heads),
        out_shape=jax.ShapeDtypeStruct((s, hdv), bf16),
        grid=(s // t,),
        in_specs=[row(hdk), row(hdk), row(hdk), row(hdv), row(hdv), _resident((1, hdv // heads))],
        out_specs=row(hdv),
        scratch_shapes=[pltpu.VMEM((heads, hdk // heads, hdv // heads), f32)],
        compiler_params=_params("arbitrary"),
        name="gla_core",
    )(q, k, g, v, r, onorm_g.reshape(1, hdv // heads))


def _moba_in_kernel(x_ref, mod_ref, ng_ref, wqt_ref, wk_ref, wvt_ref, qg_ref, kg_ref, cos_ref, sin_ref,
                    cost_ref, sint_ref, qt_ref, k_ref, vt_ref, kmean_ref, *, heads):
    hf = _norm_mod(x_ref[...], ng_ref[...], mod_ref[1:2, :], mod_ref[0:1, :])
    h = hf.astype(bf16)
    ht = hf.T.astype(bf16)
    tm = x_ref.shape[0]
    d_attn = qt_ref.shape[0]
    hd = d_attn // heads
    half = hd // 2

    cos_t = cost_ref[0:half, :]
    sin_t = sint_ref[half:hd, :]
    yqt = _dot(wqt_ref[...], ht)
    for hh in range(heads):
        y = yqt[hh * hd:(hh + 1) * hd, :]
        yn = y * lax.rsqrt(jnp.mean(y * y, axis=0, keepdims=True) + EPS) * qg_ref[...]
        t1, t2 = yn[0:half, :], yn[half:hd, :]
        qt_ref[hh * hd:hh * hd + half, :] = t1 * cos_t - t2 * sin_t
        qt_ref[hh * hd + half:(hh + 1) * hd, :] = t2 * cos_t + t1 * sin_t

    row_blk = pl.program_id(0) * (tm // MOBA_BLOCK) + lax.broadcasted_iota(jnp.int32, (tm, hd), 0) // MOBA_BLOCK
    col = lax.broadcasted_iota(jnp.int32, (tm, hd), 1)
    extra = jnp.where(col == row_blk, 1.0, 0.0).astype(bf16)
    cos = cos_ref[...]
    sin = sin_ref[...]
    yk = _dot(h, wk_ref[...])
    for hh in range(heads):
        hs = slice(hh * hd, (hh + 1) * hd)
        y = yk[:, hs]
        yn = y * lax.rsqrt(jnp.mean(y * y, axis=-1, keepdims=True) + EPS) * kg_ref[...]
        kr = yn * cos + pltpu.roll(yn, half, 1) * sin
        k_ref[:, 2 * hh * hd:(2 * hh + 1) * hd] = kr.astype(bf16)
        k_ref[:, (2 * hh + 1) * hd:(2 * hh + 2) * hd] = extra
        for b in range(tm // MOBA_BLOCK):
            kmean_ref[b, :, hs] = jnp.mean(kr[b * MOBA_BLOCK:(b + 1) * MOBA_BLOCK, :], axis=0, keepdims=True)

    vt_ref[...] = _dot(wvt_ref[...], ht).astype(bf16)


def _moba_in(x, mod, ng, w_in, qg, kg, tables):
    s, d = x.shape
    heads = MOBA_HEADS
    d_attn = w_in.shape[1] // 3
    hd = d_attn // heads
    tm = ROW_TILE
    nb = s // MOBA_BLOCK
    assert nb <= hd
    cos_t, sin_t, cos_tt, sin_tt = tables
    row = lambda n: pl.BlockSpec((tm, n), lambda i: (i, 0))
    col = lambda n: pl.BlockSpec((n, tm), lambda i: (0, i))
    wqt = w_in[:, 0:d_attn].T.astype(bf16)
    wk = w_in[:, d_attn:2 * d_attn].astype(bf16)
    wvt = w_in[:, 2 * d_attn:].T.astype(bf16)
    return pl.pallas_call(
        functools.partial(_moba_in_kernel, heads=heads),
        out_shape=(jax.ShapeDtypeStruct((d_attn, s), f32), jax.ShapeDtypeStruct((s, 2 * d_attn), bf16),
                   jax.ShapeDtypeStruct((d_attn, s), bf16), jax.ShapeDtypeStruct((nb, 1, d_attn), f32)),
        grid=(s // tm,),
        in_specs=[row(d), _resident(mod.shape), _resident(ng.shape), _resident(wqt.shape), _resident(wk.shape),
                  _resident(wvt.shape), _resident((hd, 1)), _resident((1, hd)), row(hd), row(hd), col(hd), col(hd)],
        out_specs=(col(d_attn), row(2 * d_attn), col(d_attn),
                   pl.BlockSpec((tm // MOBA_BLOCK, 1, d_attn), lambda i: (i, 0, 0))),
        compiler_params=_params("arbitrary"),
        name="moba_in_proj",
    )(x, mod, ng, wqt, wk, wvt, qg.reshape(hd, 1), kg.reshape(1, hd), cos_t, sin_t, cos_tt, sin_tt)


def _moba_gate_bias(kmean_ref, qt, own_blk, off_value):
    nb = kmean_ref.shape[0]
    tq = qt.shape[1]
    neg = jnp.float32(-jnp.inf)
    km = kmean_ref[...]
    km_hi = km.astype(bf16)
    km_lo = (km - km_hi.astype(f32)).astype(bf16)
    qt_hi = qt.astype(bf16)
    qt_lo = (qt - qt_hi.astype(f32)).astype(bf16)
    gate = _dot(km_hi, qt_hi) + _dot(km_lo, qt_hi) + _dot(km_hi, qt_lo)
    blk = lax.broadcasted_iota(jnp.int32, (nb, tq), 0)
    gate = jnp.where(blk < own_blk, gate, neg)
    bias = jnp.full((nb, tq), off_value, f32)
    for _ in range(MOBA_TOPK):
        top = jnp.max(gate, axis=0, keepdims=True)
        is_top = jnp.logical_and(gate == top, top > neg)
        first = jnp.min(jnp.where(is_top, blk, nb), axis=0, keepdims=True)
        pick = blk == first
        bias = jnp.where(pick, 0.0, bias)
        gate = jnp.where(pick, neg, gate)
    return bias


def _causal_own_scores(k_ref, qb, blk):
    bs = MOBA_BLOCK
    hd = qb.shape[0]
    st = _dot(k_ref[pl.ds(pl.multiple_of(blk * bs, bs), bs), 0:hd], qb)
    key_pos = lax.broadcasted_iota(jnp.int32, (bs, bs), 0)
    qry_pos = lax.broadcasted_iota(jnp.int32, (bs, bs), 1)
    return jnp.where(key_pos <= qry_pos, st, jnp.float32(-jnp.inf))


def _moba_attn_fast_kernel(qt_ref, k_ref, vt_ref, kmean_ref, o_ref, acc_ref, l_ref, qaug_ref):
    hd, tq = qt_ref.shape
    bs = MOBA_BLOCK
    nt = tq // bs
    nb = kmean_ref.shape[0]
    first = pl.program_id(1) * nt
    qt = qt_ref[...]
    own_blk = first + lax.broadcasted_iota(jnp.int32, (1, tq), 1) // bs
    bias = _moba_gate_bias(kmean_ref, qt, own_blk, MOBA_MASK_VALUE)
    blk = lax.broadcasted_iota(jnp.int32, (nb, tq), 0)
    bias = jnp.where(blk == own_blk, 0.0, bias)
    qaug_ref[0:hd, :] = (qt * (hd ** -0.5 * LOG2E)).astype(bf16)
    qaug_ref[hd:hd + nb, :] = bias.astype(bf16)
    qaug_ref[hd + nb:2 * hd, :] = jnp.zeros((hd - nb, tq), bf16)
    acc_ref[...] = jnp.zeros_like(acc_ref)
    l_ref[...] = jnp.zeros_like(l_ref)

    def group(g, nblk, causal):
        gk = nblk * bs
        off = pl.multiple_of(g * gk, gk)
        sg = _dot(k_ref[pl.ds(off, gk), :], qaug_ref[...])
        if causal:
            key_pos = off + lax.broadcasted_iota(jnp.int32, (gk, tq), 0)
            qry_pos = first * bs + lax.broadcasted_iota(jnp.int32, (gk, tq), 1)
            sg = jnp.where(key_pos <= qry_pos, sg, jnp.float32(-jnp.inf))
        pg = jnp.exp2(sg)
        l_ref[...] += jnp.sum(pg, axis=0, keepdims=True)
        acc_ref[...] += _dot(vt_ref[:, pl.ds(off, gk)], pg.astype(bf16))

    def past_group(g, carry):
        group(g, MOBA_GROUP, False)
        return carry

    def own_group(g, carry):
        group(g, nt, True)
        return carry

    n_past = first // MOBA_GROUP
    lax.fori_loop(0, n_past, past_group, 0)
    lax.fori_loop(n_past * (MOBA_GROUP // nt), first // nt + 1, own_group, 0)
    o_ref[...] = (acc_ref[...] / l_ref[...]).T.astype(bf16)


def _moba_attn_general_kernel(qt_ref, k_ref, vt_ref, kmean_ref, o_ref, acc_ref, l_ref, m_ref, bias_ref):
    hd, tq = qt_ref.shape
    bs = MOBA_BLOCK
    nt = tq // bs
    for a in range(nt):
        ls = slice(a * bs, (a + 1) * bs)
        i = pl.program_id(1) * nt + a
        qt = qt_ref[:, ls]
        bias_ref[...] = _moba_gate_bias(kmean_ref, qt, i, -jnp.inf)
        qb = (qt * (hd ** -0.5 * LOG2E)).astype(bf16)
        st = _causal_own_scores(k_ref, qb, i)
        m0 = jnp.max(st, axis=0, keepdims=True)
        p = jnp.exp2(st - m0)
        m_ref[...] = m0
        l_ref[:, ls] = jnp.sum(p, axis=0, keepdims=True)
        own = pl.multiple_of(i * bs, bs)
        acc_ref[:, ls] = _dot(vt_ref[:, pl.ds(own, bs)], p.astype(bf16))

        def past_block(j, carry, ls=ls, qb=qb):
            off = pl.multiple_of(j * bs, bs)
            sj = _dot(k_ref[pl.ds(off, bs), 0:hd], qb) + bias_ref[pl.ds(j, 1), :]
            m_old = m_ref[...]
            m_new = jnp.maximum(m_old, jnp.max(sj, axis=0, keepdims=True))
            alpha = jnp.exp2(m_old - m_new)
            pj = jnp.exp2(sj - m_new)
            l_ref[:, ls] = alpha * l_ref[:, ls] + jnp.sum(pj, axis=0, keepdims=True)
            acc_ref[:, ls] = alpha * acc_ref[:, ls] + _dot(vt_ref[:, pl.ds(off, bs)], pj.astype(bf16))
            m_ref[...] = m_new
            return carry

        lax.fori_loop(0, i, past_block, 0)
        o_ref[ls, :] = (acc_ref[:, ls] / l_ref[:, ls]).T.astype(bf16)


def _moba_attn(qt, k, vt, kmean, qg, kg):
    d_attn, s = qt.shape
    heads = MOBA_HEADS
    hd = d_attn // heads
    tq = MOBA_QTILE
    bs = MOBA_BLOCK
    nb = s // bs
    assert nb % MOBA_GROUP == 0 and s % tq == 0 and tq % bs == 0 and MOBA_GROUP % (tq // bs) == 0

    def call(body, scratch):
        return pl.pallas_call(
            body,
            out_shape=jax.ShapeDtypeStruct((s, d_attn), bf16),
            grid=(heads, s // tq),
            in_specs=[
                pl.BlockSpec((hd, tq), lambda h, t: (h, t)),
                pl.BlockSpec((s, 2 * hd), lambda h, t: (0, h)),
                pl.BlockSpec((hd, s), lambda h, t: (h, 0)),
                pl.BlockSpec((nb, hd), lambda h, t: (0, h)),
            ],
            out_specs=pl.BlockSpec((tq, hd), lambda h, t: (t, h)),
            scratch_shapes=scratch,
            compiler_params=_params("arbitrary", "arbitrary"),
            name=body.__name__.strip("_"),
        )(qt, k, vt, kmean)

    stats = [pltpu.VMEM((hd, tq), f32), pltpu.VMEM((1, tq), f32)]
    fast = lambda: call(_moba_attn_fast_kernel, stats + [pltpu.VMEM((2 * hd, tq), bf16)])
    general = lambda: call(_moba_attn_general_kernel,
                           stats + [pltpu.VMEM((1, bs), f32), pltpu.VMEM((nb, bs), f32)])
    bound = 1.02 * hd ** 0.5 * LOG2E * jnp.max(jnp.abs(qg)) * jnp.max(jnp.abs(kg))
    return lax.cond(bound <= MOBA_FAST_RANGE, fast, general)


def _out_mlp_kernel(a_ref, x_ref, mod_ref, ng_ref, wo_ref, w1_ref, w2_ref, o_ref, acc_ref):
    x1 = x_ref[...] + mod_ref[2:3, :] * _dot(a_ref[...], wo_ref[...].astype(bf16))
    h = _norm_mod(x1, ng_ref[...], mod_ref[4:5, :], mod_ref[3:4, :]).astype(bf16)
    d_ff = w1_ref.shape[1]
    for c in range(d_ff // FF_CHUNK):
        cs = slice(c * FF_CHUNK, (c + 1) * FF_CHUNK)
        u = jnp.maximum(_dot(h, w1_ref[:, cs].astype(bf16)), 0.0)
        part = _dot((u * u).astype(bf16), w2_ref[cs, :].astype(bf16))
        if c == 0:
            acc_ref[...] = part
        else:
            acc_ref[...] += part
    o_ref[...] = x1 + mod_ref[5:6, :] * acc_ref[...]


def _out_mlp(a, x, mod, ng, w_out_all, mixer_layer, w1_all, w2_all, layer):
    s, d = x.shape
    tm = ROW_TILE
    row = lambda n: pl.BlockSpec((tm, n), lambda i: (i, 0))
    return pl.pallas_call(
        _out_mlp_kernel,
        out_shape=jax.ShapeDtypeStruct((s, d), f32),
        grid=(s // tm,),
        in_specs=[row(a.shape[1]), row(d), _resident(mod.shape), _resident(ng.shape),
                  _resident_layer(w_out_all.shape, mixer_layer), _resident_layer(w1_all.shape, layer),
                  _resident_layer(w2_all.shape, layer)],
        out_specs=row(d),
        scratch_shapes=[pltpu.VMEM((tm, d), f32)],
        compiler_params=_params("arbitrary"),
        name="out_proj_mlp",
    )(a, x, mod, ng, w_out_all, w1_all, w2_all)


def kernel(x, c, positions, ada_w, ada_b, norm_mix_g, norm_mlp_g, gla_w_in, gla_w_gate_up, gla_b_gate,
           gla_onorm_g, gla_w_out, moba_w_in, moba_q_norm_g, moba_k_norm_g, moba_w_out, mlp_w1, mlp_w2):
    b, s, d = x.shape
    assert b == 1 and s % ROW_TILE == 0 and s % MOBA_BLOCK == 0
    depth = ada_w.shape[0]
    xs = x.reshape(s, d)
    mod = _ada_mod(c, ada_w, ada_b)
    moba_hd = moba_w_out.shape[1] // MOBA_HEADS
    tables = _rope_tables(positions, moba_hd)
    for i in range(depth):
        j = i // 2
        ng = norm_mix_g[i].reshape(1, d)
        if i % 2 == 0:
            q, k, v, r, g = _gla_in(xs, mod[i], ng, gla_w_in, j, gla_w_gate_up[j], gla_b_gate[j])
            a = _gla_core(q, k, g, v, r, gla_onorm_g[j])
            w_out = gla_w_out
        else:
            q, k, vt, kmean = _moba_in(xs, mod[i], ng, moba_w_in[j], moba_q_norm_g[j], moba_k_norm_g[j], tables)
            a = _moba_attn(q, k, vt, kmean.reshape(kmean.shape[0], kmean.shape[2]),
                           moba_q_norm_g[j], moba_k_norm_g[j])
            w_out = moba_w_out
        xs = _out_mlp(a, xs, mod[i], norm_mlp_g[i].reshape(1, d), w_out, j, mlp_w1, mlp_w2, i)
    return xs.reshape(b, s, d)
```

```python
import functools

import jax
import jax.numpy as jnp
from jax import lax
from jax.experimental import pallas as pl
from jax.experimental.pallas import tpu as pltpu

f32 = jnp.float32
bf16 = jnp.bfloat16

EPS = 1e-6
GLA_HEADS = 4
GLA_GATE_RANK = 16
GLA_GATE_NORM = 16.0
GLA_CHUNK = 64
MOBA_HEADS = 8
MOBA_BLOCK = 256
MOBA_TOPK = 3
MOBA_GROUP = 8
MOBA_QTILE = 1024
LOG2E = 1.4426950408889634
MOBA_MASK_VALUE = -2.0 ** 30
MOBA_FAST_RANGE = 32.0
ROPE_THETA = 10000.0

LANES = 128
VMEM_LIMIT = 56 * 1024 * 1024

ROW_TILE = 512
GLA_STEP = 256
FF_CHUNK = 1024


def _params(*sem):
    return pltpu.CompilerParams(dimension_semantics=sem, vmem_limit_bytes=VMEM_LIMIT)


def _resident(shape):
    nd = len(shape)
    return pl.BlockSpec(shape, lambda *_: (0,) * nd, pipeline_mode=pl.Buffered(1))


def _resident_layer(stacked_shape, layer):
    nd = len(stacked_shape) - 1
    return pl.BlockSpec((None,) + tuple(stacked_shape[1:]), lambda *_: (layer,) + (0,) * nd,
                        pipeline_mode=pl.Buffered(1))


def _dot(a, b):
    return jnp.dot(a, b, preferred_element_type=f32)


def _dot_nt(a, b):
    return lax.dot_general(a, b, (((1,), (1,)), ((), ())), preferred_element_type=f32)


def _split2(x):
    hi = x.astype(bf16)
    return hi, (x - hi.astype(f32)).astype(bf16)


def _silu(x):
    return x / (1.0 + jnp.exp(-x))


def _norm_mod(x, g, scale, shift):
    ms = jnp.mean(x * x, axis=-1, keepdims=True)
    y = x * lax.rsqrt(ms + EPS)
    return (y * g) * (1.0 + scale) + shift


def _ada_kernel(c_ref, w_ref, b_ref, o_ref):
    ca = _silu(c_ref[...])
    o_ref[...] = jnp.sum(ca * w_ref[...], axis=0, keepdims=True) + b_ref[...]


def _ada_mod(c, ada_w, ada_b):
    depth, d, n = ada_w.shape
    tn = 1536
    out = pl.pallas_call(
        _ada_kernel,
        out_shape=jax.ShapeDtypeStruct((depth, 1, n), f32),
        grid=(depth, n // tn),
        in_specs=[
            pl.BlockSpec((d, 1), lambda l, j: (0, 0)),
            pl.BlockSpec((None, d, tn), lambda l, j: (l, 0, j)),
            pl.BlockSpec((None, 1, tn), lambda l, j: (l, 0, j)),
        ],
        out_specs=pl.BlockSpec((None, 1, tn), lambda l, j: (l, 0, j)),
        compiler_params=_params("arbitrary", "arbitrary"),
        name="ada_mod",
    )(c.reshape(d, 1), ada_w, ada_b.reshape(depth, 1, n))
    return out.reshape(depth, 6, d)


def _rope_table_kernel(pos_ref, freq_ref, sign_ref, cos_ref, sin_ref, cost_ref, sint_ref):
    ang = pos_ref[...].astype(f32) * freq_ref[...]
    cos = jnp.cos(ang)
    sin = jnp.sin(ang) * sign_ref[...]
    cos_ref[...] = cos
    sin_ref[...] = sin
    cost_ref[...] = cos.T
    sint_ref[...] = sin.T


def _rope_tables(positions, hd):
    s = positions.shape[-1]
    half = hd // 2
    inv_freq = ROPE_THETA ** (-jnp.arange(half, dtype=f32) / half)
    freq = jnp.concatenate([inv_freq, inv_freq]).reshape(1, hd)
    sign = jnp.concatenate([-jnp.ones((half,), f32), jnp.ones((half,), f32)]).reshape(1, hd)
    tm = ROW_TILE
    row = pl.BlockSpec((tm, hd), lambda i: (i, 0))
    col = pl.BlockSpec((hd, tm), lambda i: (0, i))
    return pl.pallas_call(
        _rope_table_kernel,
        out_shape=(jax.ShapeDtypeStruct((s, hd), f32), jax.ShapeDtypeStruct((s, hd), f32),
                   jax.ShapeDtypeStruct((hd, s), f32), jax.ShapeDtypeStruct((hd, s), f32)),
        grid=(s // tm,),
        in_specs=[
            pl.BlockSpec((tm, 1), lambda i: (i, 0)),
            pl.BlockSpec((1, hd), lambda i: (0, 0)),
            pl.BlockSpec((1, hd), lambda i: (0, 0)),
        ],
        out_specs=(row, row, col, col),
        compiler_params=_params("arbitrary"),
        name="rope_tables",
    )(positions.reshape(s, 1), freq, sign)


def _gla_in_kernel(x_ref, mod_ref, ng_ref, w_ref, wglr_ref, wup_ref, bg_ref,
                   q_ref, k_ref, v_ref, r_ref, g_ref, *, hdk, hdv, rank):
    h = _norm_mod(x_ref[...], ng_ref[...], mod_ref[1:2, :], mod_ref[0:1, :]).astype(bf16)
    glr3 = _dot(h, wglr_ref[...])
    hi = glr3.astype(bf16).astype(f32)
    lane = lax.broadcasted_iota(jnp.int32, glr3.shape, 1)
    use_lo = jnp.logical_and(lane >= rank, lane < 2 * rank)
    z = _dot(jnp.where(use_lo, glr3 - hi, hi).astype(bf16), wup_ref[...]) + bg_ref[...]
    log_sig = jnp.minimum(z, 0.0) - jnp.log(1.0 + jnp.exp(-jnp.abs(z)))
    g_ref[...] = log_sig / GLA_GATE_NORM
    q_ref[...] = _dot(h, w_ref[:, 0:hdk].astype(bf16))
    k_ref[...] = _dot(h, w_ref[:, hdk:2 * hdk].astype(bf16))
    v_ref[...] = _dot(h, w_ref[:, 2 * hdk:2 * hdk + hdv].astype(bf16))
    r_ref[...] = _dot(h, w_ref[:, 2 * hdk + hdv:2 * hdk + 2 * hdv].astype(bf16))


def _gla_in(x, mod, ng, w_in_all, layer, w_gate_up, b_gate):
    s, d = x.shape
    w_in = w_in_all[layer]
    hdk = w_gate_up.shape[1]
    rank = w_gate_up.shape[0]
    hdv = (w_in.shape[1] - 2 * hdk - rank) // 2
    n_main = 2 * hdk + 2 * hdv
    assert 3 * rank <= LANES
    w_glr = jnp.pad(jnp.tile(w_in[:, n_main:], (1, 3)), ((0, 0), (0, LANES - 3 * rank))).astype(bf16)
    wup_hi = w_gate_up.astype(bf16)
    wup_lo = (w_gate_up - wup_hi.astype(f32)).astype(bf16)
    wup = jnp.pad(jnp.concatenate([wup_hi, wup_hi, wup_lo], axis=0), ((0, LANES - 3 * rank), (0, 0)))
    tm = ROW_TILE
    row = lambda n: pl.BlockSpec((tm, n), lambda i: (i, 0))
    return pl.pallas_call(
        functools.partial(_gla_in_kernel, hdk=hdk, hdv=hdv, rank=rank),
        out_shape=(jax.ShapeDtypeStruct((s, hdk), f32), jax.ShapeDtypeStruct((s, hdk), f32),
                   jax.ShapeDtypeStruct((s, hdv), f32), jax.ShapeDtypeStruct((s, hdv), f32),
                   jax.ShapeDtypeStruct((s, hdk), f32)),
        grid=(s // tm,),
        in_specs=[row(d), _resident(mod.shape), _resident(ng.shape), _resident_layer(w_in_all.shape, layer),
                  _resident(w_glr.shape), _resident(wup.shape), _resident((1, hdk))],
        out_specs=(row(hdk), row(hdk), row(hdv), row(hdv), row(hdk)),
        compiler_params=_params("arbitrary"),
        name="gla_in_proj",
    )(x, mod, ng, w_in_all, w_glr, wup, b_gate.reshape(1, hdk))


def _gla_core_kernel(q_ref, k_ref, g_ref, v_ref, r_ref, og_ref, o_ref, s_ref, *, heads):
    @pl.when(pl.program_id(0) == 0)
    def _():
        s_ref[...] = jnp.zeros_like(s_ref)

    t = q_ref.shape[0]
    dk = q_ref.shape[1] // heads
    dv = v_ref.shape[1] // heads
    lc = GLA_CHUNK
    nch = t // lc
    row = lax.broadcasted_iota(jnp.int32, (t, t), 0)
    col = lax.broadcasted_iota(jnp.int32, (t, t), 1)
    tri = jnp.logical_and(col <= row, (row // lc) == (col // lc))
    tri_b = jnp.where(tri, 1.0, 0.0).astype(bf16)
    lane_chunk = lax.broadcasted_iota(jnp.int32, (dk, t), 1) // lc
    row_chunk = lax.broadcasted_iota(jnp.int32, (t, dk), 0) // lc
    scale = dk ** -0.5

    g_hi, g_lo = _split2(g_ref[...])
    gc = _dot(tri_b, g_hi) + _dot(tri_b, g_lo)
    g_last = jnp.concatenate(
        [jnp.broadcast_to(gc[c * lc + lc - 1:c * lc + lc, :], (lc, heads * dk)) for c in range(nch)], axis=0)
    k_all = k_ref[...]
    q_dec_all = q_ref[...] * scale * jnp.exp(gc)
    k_inv_all = (k_all * jnp.exp(-gc)).astype(bf16)
    k_tail_t_all = (k_all * jnp.exp(g_last - gc)).T
    gc_t_all = gc.T

    for h in range(heads):
        ks = slice(h * dk, (h + 1) * dk)
        vs = slice(h * dv, (h + 1) * dv)
        q_dec = q_dec_all[:, ks]
        k_tail_t = k_tail_t_all[ks, :]
        vh = v_ref[:, vs].astype(bf16)
        a = jnp.where(tri, _dot_nt(q_dec.astype(bf16), k_inv_all[:, ks]), 0.0).astype(bf16)
        kt_stack = jnp.concatenate(
            [jnp.where(lane_chunk == c, k_tail_t, 0.0).astype(bf16) for c in range(nch)], axis=0)
        u_all = _dot(kt_stack, vh)
        state = s_ref[h]
        states = []
        for c in range(nch):
            states.append(state.astype(bf16))
            decay = jnp.exp(gc_t_all[ks, c * lc + lc - 1:c * lc + lc])
            state = decay * state + u_all[c * dk:(c + 1) * dk, :]
        s_ref[h] = state
        q_blocks = [jnp.where(row_chunk == c, q_dec, 0.0).astype(bf16) for c in range(nch)]
        o = _dot(jnp.concatenate([a] + q_blocks, axis=1), jnp.concatenate([vh] + states, axis=0))
        ms = jnp.mean(o * o, axis=-1, keepdims=True)
        on = o * lax.rsqrt(ms + EPS) * og_ref[...]
        o_ref[:, vs] = (on * _silu(r_ref[:, vs])).astype(bf16)


def _gla_core(q, k, g, v, r, onorm_g):
    s, hdk = q.shape
    hdv = v.shape[1]
    heads = GLA_HEADS
    t = GLA_STEP
    row = lambda n: pl.BlockSpec((t, n), lambda i: (i, 0))
    return pl.pallas_call(
        functools.partial(_gla_core_kernel, heads=heads),
        out_shape=jax.ShapeDtypeStruct((s, hdv), bf16),
        grid=(s // t,),
        in_specs=[row(hdk), row(hdk), row(hdk), row(hdv), row(hdv), _resident((1, hdv // heads))],
        out_specs=row(hdv),
        scratch_shapes=[pltpu.VMEM((heads, hdk // heads, hdv // heads), f32)],
        compiler_params=_params("arbitrary"),
        name="gla_core",
    )(q, k, g, v, r, onorm_g.reshape(1, hdv // heads))


def _moba_in_kernel(x_ref, mod_ref, ng_ref, wqt_ref, wk_ref, wvt_ref, qg_ref, kg_ref, cos_ref, sin_ref,
                    cost_ref, sint_ref, qt_ref, k_ref, vt_ref, kmean_ref, yq_ref, yk_ref, *, heads):
    i = pl.program_id(0)
    tm = x_ref.shape[0]
    d_attn = qt_ref.shape[0]
    hd = d_attn // heads
    half = hd // 2

    @pl.when(i == 0)
    def _():
        yq_ref[1] = jnp.zeros(yq_ref.shape[1:], f32)
        yk_ref[1] = jnp.zeros(yk_ref.shape[1:], f32)

    def step(cur, prev):
        hf = _norm_mod(x_ref[...], ng_ref[...], mod_ref[1:2, :], mod_ref[0:1, :])
        h = hf.astype(bf16)
        ht = hf.T.astype(bf16)
        yq_ref[cur] = _dot(wqt_ref[...], ht)
        yk_ref[cur] = _dot(h, wk_ref[...])
        vt_ref[...] = _dot(wvt_ref[...], ht).astype(bf16)

        cos_t = cost_ref[0:half, :]
        sin_t = sint_ref[half:hd, :]
        for hh in range(heads):
            y = yq_ref[prev, hh * hd:(hh + 1) * hd, :]
            yn = y * lax.rsqrt(jnp.mean(y * y, axis=0, keepdims=True) + EPS) * qg_ref[...]
            t1, t2 = yn[0:half, :], yn[half:hd, :]
            qt_ref[hh * hd:hh * hd + half, :] = t1 * cos_t - t2 * sin_t
            qt_ref[hh * hd + half:(hh + 1) * hd, :] = t2 * cos_t + t1 * sin_t

        row_blk = (jnp.maximum(i - 1, 0) * (tm // MOBA_BLOCK)
                   + lax.broadcasted_iota(jnp.int32, (tm, hd), 0) // MOBA_BLOCK)
        col = lax.broadcasted_iota(jnp.int32, (tm, hd), 1)
        extra = jnp.where(col == row_blk, 1.0, 0.0).astype(bf16)
        cos = cos_ref[...]
        sin = sin_ref[...]
        for hh in range(heads):
            hs = slice(hh * hd, (hh + 1) * hd)
            y = yk_ref[prev, :, hs]
            yn = y * lax.rsqrt(jnp.mean(y * y, axis=-1, keepdims=True) + EPS) * kg_ref[...]
            kr = yn * cos + pltpu.roll(yn, half, 1) * sin
            k_ref[:, 2 * hh * hd:(2 * hh + 1) * hd] = kr.astype(bf16)
            k_ref[:, (2 * hh + 1) * hd:(2 * hh + 2) * hd] = extra
            for b in range(tm // MOBA_BLOCK):
                kmean_ref[b, :, hs] = jnp.mean(kr[b * MOBA_BLOCK:(b + 1) * MOBA_BLOCK, :], axis=0, keepdims=True)

    @pl.when(i % 2 == 0)
    def _():
        step(0, 1)

    @pl.when(i % 2 == 1)
    def _():
        step(1, 0)


def _moba_in(x, mod, ng, w_in, qg, kg, tables):
    s, d = x.shape
    heads = MOBA_HEADS
    d_attn = w_in.shape[1] // 3
    hd = d_attn // heads
    tm = ROW_TILE
    nb = s // MOBA_BLOCK
    n = s // tm
    assert nb <= hd
    cos_t, sin_t, cos_tt, sin_tt = tables
    cur = lambda i: jnp.minimum(i, n - 1)
    prev = lambda i: jnp.maximum(i - 1, 0)
    wqt = w_in[:, 0:d_attn].T.astype(bf16)
    wk = w_in[:, d_attn:2 * d_attn].astype(bf16)
    wvt = w_in[:, 2 * d_attn:].T.astype(bf16)
    return pl.pallas_call(
        functools.partial(_moba_in_kernel, heads=heads),
        out_shape=(jax.ShapeDtypeStruct((d_attn, s), f32), jax.ShapeDtypeStruct((s, 2 * d_attn), bf16),
                   jax.ShapeDtypeStruct((d_attn, s), bf16), jax.ShapeDtypeStruct((nb, 1, d_attn), f32)),
        grid=(n + 1,),
        in_specs=[pl.BlockSpec((tm, d), lambda i: (cur(i), 0)), _resident(mod.shape), _resident(ng.shape),
                  _resident(wqt.shape), _resident(wk.shape), _resident(wvt.shape), _resident((hd, 1)),
                  _resident((1, hd)),
                  pl.BlockSpec((tm, hd), lambda i: (prev(i), 0)), pl.BlockSpec((tm, hd), lambda i: (prev(i), 0)),
                  pl.BlockSpec((hd, tm), lambda i: (0, prev(i))), pl.BlockSpec((hd, tm), lambda i: (0, prev(i)))],
        out_specs=(pl.BlockSpec((d_attn, tm), lambda i: (0, prev(i))),
                   pl.BlockSpec((tm, 2 * d_attn), lambda i: (prev(i), 0)),
                   pl.BlockSpec((d_attn, tm), lambda i: (0, cur(i))),
                   pl.BlockSpec((tm // MOBA_BLOCK, 1, d_attn), lambda i: (prev(i), 0, 0))),
        scratch_shapes=[pltpu.VMEM((2, d_attn, tm), f32), pltpu.VMEM((2, tm, d_attn), f32)],
        compiler_params=_params("arbitrary"),
        name="moba_in_proj",
    )(x, mod, ng, wqt, wk, wvt, qg.reshape(hd, 1), kg.reshape(1, hd), cos_t, sin_t, cos_tt, sin_tt)


def _moba_gate_bias(kmean_ref, qt, own_blk, off_value):
    nb = kmean_ref.shape[0]
    tq = qt.shape[1]
    neg = jnp.float32(-jnp.inf)
    km = kmean_ref[...]
    km_hi = km.astype(bf16)
    km_lo = (km - km_hi.astype(f32)).astype(bf16)
    qt_hi = qt.astype(bf16)
    qt_lo = (qt - qt_hi.astype(f32)).astype(bf16)
    gate = _dot(km_hi, qt_hi) + _dot(km_lo, qt_hi) + _dot(km_hi, qt_lo)
    blk = lax.broadcasted_iota(jnp.int32, (nb, tq), 0)
    gate = jnp.where(blk < own_blk, gate, neg)
    bias = jnp.full((nb, tq), off_value, f32)
    for _ in range(MOBA_TOPK):
        top = jnp.max(gate, axis=0, keepdims=True)
        is_top = jnp.logical_and(gate == top, top > neg)
        first = jnp.min(jnp.where(is_top, blk, nb), axis=0, keepdims=True)
        pick = blk == first
        bias = jnp.where(pick, 0.0, bias)
        gate = jnp.where(pick, neg, gate)
    return bias


def _causal_own_scores(k_ref, qb, blk):
    bs = MOBA_BLOCK
    hd = qb.shape[0]
    st = _dot(k_ref[pl.ds(pl.multiple_of(blk * bs, bs), bs), 0:hd], qb)
    key_pos = lax.broadcasted_iota(jnp.int32, (bs, bs), 0)
    qry_pos = lax.broadcasted_iota(jnp.int32, (bs, bs), 1)
    return jnp.where(key_pos <= qry_pos, st, jnp.float32(-jnp.inf))


def _moba_attn_fast_kernel(qt_ref, k_ref, vt_ref, kmean_ref, o_ref, acc_ref, l_ref, qaug_ref):
    hd, tq = qt_ref.shape
    bs = MOBA_BLOCK
    nt = tq // bs
    nb = kmean_ref.shape[0]
    first = pl.program_id(1) * nt
    qt = qt_ref[...]
    own_blk = first + lax.broadcasted_iota(jnp.int32, (1, tq), 1) // bs
    bias = _moba_gate_bias(kmean_ref, qt, own_blk, MOBA_MASK_VALUE)
    blk = lax.broadcasted_iota(jnp.int32, (nb, tq), 0)
    bias = jnp.where(blk == own_blk, 0.0, bias)
    qaug_ref[0:hd, :] = (qt * (hd ** -0.5 * LOG2E)).astype(bf16)
    qaug_ref[hd:hd + nb, :] = bias.astype(bf16)
    qaug_ref[hd + nb:2 * hd, :] = jnp.zeros((hd - nb, tq), bf16)
    acc_ref[...] = jnp.zeros_like(acc_ref)
    l_ref[...] = jnp.zeros_like(l_ref)

    def group(g, nblk, causal):
        gk = nblk * bs
        off = pl.multiple_of(g * gk, gk)
        sg = _dot(k_ref[pl.ds(off, gk), :], qaug_ref[...])
        if causal:
            key_pos = off + lax.broadcasted_iota(jnp.int32, (gk, tq), 0)
            qry_pos = first * bs + lax.broadcasted_iota(jnp.int32, (gk, tq), 1)
            sg = jnp.where(key_pos <= qry_pos, sg, jnp.float32(-jnp.inf))
        pg = jnp.exp2(sg)
        l_ref[...] += jnp.sum(pg, axis=0, keepdims=True)
        acc_ref[...] += _dot(vt_ref[:, pl.ds(off, gk)], pg.astype(bf16))

    def past_group(g, carry):
        group(g, MOBA_GROUP, False)
        return carry

    def own_group(g, carry):
        group(g, nt, True)
        return carry

    n_past = first // MOBA_GROUP
    lax.fori_loop(0, n_past, past_group, 0)
    lax.fori_loop(n_past * (MOBA_GROUP // nt), first // nt + 1, own_group, 0)
    o_ref[...] = (acc_ref[...] / l_ref[...]).T.astype(bf16)


def _moba_attn_general_kernel(qt_ref, k_ref, vt_ref, kmean_ref, o_ref, acc_ref, l_ref, m_ref, bias_ref):
    hd, tq = qt_ref.shape
    bs = MOBA_BLOCK
    nt = tq // bs
    for a in range(nt):
        ls = slice(a * bs, (a + 1) * bs)
        i = pl.program_id(1) * nt + a
        qt = qt_ref[:, ls]
        bias_ref[...] = _moba_gate_bias(kmean_ref, qt, i, -jnp.inf)
        qb = (qt * (hd ** -0.5 * LOG2E)).astype(bf16)
        st = _causal_own_scores(k_ref, qb, i)
        m0 = jnp.max(st, axis=0, keepdims=True)
        p = jnp.exp2(st - m0)
        m_ref[...] = m0
        l_ref[:, ls] = jnp.sum(p, axis=0, keepdims=True)
        own = pl.multiple_of(i * bs, bs)
        acc_ref[:, ls] = _dot(vt_ref[:, pl.ds(own, bs)], p.astype(bf16))

        def past_block(j, carry, ls=ls, qb=qb):
            off = pl.multiple_of(j * bs, bs)
            sj = _dot(k_ref[pl.ds(off, bs), 0:hd], qb) + bias_ref[pl.ds(j, 1), :]
            m_old = m_ref[...]
            m_new = jnp.maximum(m_old, jnp.max(sj, axis=0, keepdims=True))
            alpha = jnp.exp2(m_old - m_new)
            pj = jnp.exp2(sj - m_new)
            l_ref[:, ls] = alpha * l_ref[:, ls] + jnp.sum(pj, axis=0, keepdims=True)
            acc_ref[:, ls] = alpha * acc_ref[:, ls] + _dot(vt_ref[:, pl.ds(off, bs)], pj.astype(bf16))
            m_ref[...] = m_new
            return carry

        lax.fori_loop(0, i, past_block, 0)
        o_ref[ls, :] = (acc_ref[:, ls] / l_ref[:, ls]).T.astype(bf16)


def _moba_attn(qt, k, vt, kmean, qg, kg):
    d_attn, s = qt.shape
    heads = MOBA_HEADS
    hd = d_attn // heads
    tq = MOBA_QTILE
    bs = MOBA_BLOCK
    nb = s // bs
    assert nb % MOBA_GROUP == 0 and s % tq == 0 and tq % bs == 0 and MOBA_GROUP % (tq // bs) == 0

    def call(body, scratch):
        return pl.pallas_call(
            body,
            out_shape=jax.ShapeDtypeStruct((s, d_attn), bf16),
            grid=(heads, s // tq),
            in_specs=[
                pl.BlockSpec((hd, tq), lambda h, t: (h, t)),
                pl.BlockSpec((s, 2 * hd), lambda h, t: (0, h)),
                pl.BlockSpec((hd, s), lambda h, t: (h, 0)),
                pl.BlockSpec((nb, hd), lambda h, t: (0, h)),
            ],
            out_specs=pl.BlockSpec((tq, hd), lambda h, t: (t, h)),
            scratch_shapes=scratch,
            compiler_params=_params("arbitrary", "arbitrary"),
            name=body.__name__.strip("_"),
        )(qt, k, vt, kmean)

    stats = [pltpu.VMEM((hd, tq), f32), pltpu.VMEM((1, tq), f32)]
    fast = lambda: call(_moba_attn_fast_kernel, stats + [pltpu.VMEM((2 * hd, tq), bf16)])
    general = lambda: call(_moba_attn_general_kernel,
                           stats + [pltpu.VMEM((1, bs), f32), pltpu.VMEM((nb, bs), f32)])
    bound = 1.02 * hd ** 0.5 * LOG2E * jnp.max(jnp.abs(qg)) * jnp.max(jnp.abs(kg))
    return lax.cond(bound <= MOBA_FAST_RANGE, fast, general)


def _out_mlp_kernel(a_ref, x_ref, mod_ref, ng_ref, wo_ref, w1_ref, w2_ref, o_ref, acc_ref):
    x1 = x_ref[...] + mod_ref[2:3, :] * _dot(a_ref[...], wo_ref[...].astype(bf16))
    h = _norm_mod(x1, ng_ref[...], mod_ref[4:5, :], mod_ref[3:4, :]).astype(bf16)
    d_ff = w1_ref.shape[1]
    for c in range(d_ff // FF_CHUNK):
        cs = slice(c * FF_CHUNK, (c + 1) * FF_CHUNK)
        u = jnp.maximum(_dot(h, w1_ref[:, cs].astype(bf16)), 0.0)
        part = _dot((u * u).astype(bf16), w2_ref[cs, :].astype(bf16))
        if c == 0:
            acc_ref[...] = part
        else:
            acc_ref[...] += part
    o_ref[...] = x1 + mod_ref[5:6, :] * acc_ref[...]


def _out_mlp(a, x, mod, ng, w_out_all, mixer_layer, w1_all, w2_all, layer):
    s, d = x.shape
    tm = ROW_TILE
    row = lambda n: pl.BlockSpec((tm, n), lambda i: (i, 0))
    return pl.pallas_call(
        _out_mlp_kernel,
        out_shape=jax.ShapeDtypeStruct((s, d), f32),
        grid=(s // tm,),
        in_specs=[row(a.shape[1]), row(d), _resident(mod.shape), _resident(ng.shape),
                  _resident_layer(w_out_all.shape, mixer_layer), _resident_layer(w1_all.shape, layer),
                  _resident_layer(w2_all.shape, layer)],
        out_specs=row(d),
        scratch_shapes=[pltpu.VMEM((tm, d), f32)],
        compiler_params=_params("arbitrary"),
        name="out_proj_mlp",
    )(a, x, mod, ng, w_out_all, w1_all, w2_all)


def kernel(x, c, positions, ada_w, ada_b, norm_mix_g, norm_mlp_g, gla_w_in, gla_w_gate_up, gla_b_gate,
           gla_onorm_g, gla_w_out, moba_w_in, moba_q_norm_g, moba_k_norm_g, moba_w_out, mlp_w1, mlp_w2):
    b, s, d = x.shape
    assert b == 1 and s % ROW_TILE == 0 and s % MOBA_BLOCK == 0
    depth = ada_w.shape[0]
    xs = x.reshape(s, d)
    mod = _ada_mod(c, ada_w, ada_b)
    moba_hd = moba_w_out.shape[1] // MOBA_HEADS
    tables = _rope_tables(positions, moba_hd)
    for i in range(depth):
        j = i // 2
        ng = norm_mix_g[i].reshape(1, d)
        if i % 2 == 0:
            q, k, v, r, g = _gla_in(xs, mod[i], ng, gla_w_in, j, gla_w_gate_up[j], gla_b_gate[j])
            a = _gla_core(q, k, g, v, r, gla_onorm_g[j])
            w_out = gla_w_out
        else:
            q, k, vt, kmean = _moba_in(xs, mod[i], ng, moba_w_in[j], moba_q_norm_g[j], moba_k_norm_g[j], tables)
            a = _moba_attn(q, k, vt, kmean.reshape(kmean.shape[0], kmean.shape[2]),
                           moba_q_norm_g[j], moba_k_norm_g[j])
            w_out = moba_w_out
        xs = _out_mlp(a, xs, mod[i], norm_mlp_g[i].reshape(1, d), w_out, j, mlp_w1, mlp_w2, i)
    return xs.reshape(b, s, d)
```

```python
import functools

import jax
import jax.numpy as jnp
from jax import lax
from jax.experimental import pallas as pl
from jax.experimental.pallas import tpu as pltpu

f32 = jnp.float32
bf16 = jnp.bfloat16

EPS = 1e-6
GLA_HEADS = 4
GLA_GATE_RANK = 16
GLA_GATE_NORM = 16.0
GLA_CHUNK = 64
MOBA_HEADS = 8
MOBA_BLOCK = 256
MOBA_TOPK = 3
MOBA_GROUP = 8
MOBA_QTILE = 1024
LOG2E = 1.4426950408889634
MOBA_MASK_VALUE = -2.0 ** 30
MOBA_FAST_RANGE = 32.0
ROPE_THETA = 10000.0

LANES = 128
SUBLANES = 8
VMEM_LIMIT = 56 * 1024 * 1024

ROW_TILE = 512
GLA_STEP = 256
FF_CHUNK = 1024


def _params(*sem):
    return pltpu.CompilerParams(dimension_semantics=sem, vmem_limit_bytes=VMEM_LIMIT)


def _resident(shape):
    nd = len(shape)
    return pl.BlockSpec(shape, lambda *_: (0,) * nd, pipeline_mode=pl.Buffered(1))


def _resident_layer(stacked_shape, layer):
    nd = len(stacked_shape) - 1
    return pl.BlockSpec((None,) + tuple(stacked_shape[1:]), lambda *_: (layer,) + (0,) * nd,
                        pipeline_mode=pl.Buffered(1))


def _dot(a, b):
    return jnp.dot(a, b, preferred_element_type=f32)


def _dot_nt(a, b):
    return lax.dot_general(a, b, (((1,), (1,)), ((), ())), preferred_element_type=f32)


def _split2(x):
    hi = x.astype(bf16)
    return hi, (x - hi.astype(f32)).astype(bf16)


def _silu(x):
    return x / (1.0 + jnp.exp(-x))


def _norm_mod(x, g, scale, shift):
    ms = jnp.mean(x * x, axis=-1, keepdims=True)
    y = x * lax.rsqrt(ms + EPS)
    return (y * g) * (1.0 + scale) + shift


def _ada_kernel(c_ref, w_ref, b_ref, o_ref):
    ca = _silu(c_ref[...])
    o_ref[...] = jnp.sum(ca * w_ref[...], axis=0, keepdims=True) + b_ref[...]


def _ada_mod(c, ada_w, ada_b):
    depth, d, n = ada_w.shape
    tn = 1536
    out = pl.pallas_call(
        _ada_kernel,
        out_shape=jax.ShapeDtypeStruct((depth, 1, n), f32),
        grid=(depth, n // tn),
        in_specs=[
            pl.BlockSpec((d, 1), lambda l, j: (0, 0)),
            pl.BlockSpec((None, d, tn), lambda l, j: (l, 0, j)),
            pl.BlockSpec((None, 1, tn), lambda l, j: (l, 0, j)),
        ],
        out_specs=pl.BlockSpec((None, 1, tn), lambda l, j: (l, 0, j)),
        compiler_params=_params("arbitrary", "arbitrary"),
        name="ada_mod",
    )(c.reshape(d, 1), ada_w, ada_b.reshape(depth, 1, n))
    return out.reshape(depth, 6, d)


def _rope_table_kernel(pos_ref, freq_ref, sign_ref, cos_ref, sin_ref, cost_ref, sint_ref):
    ang = pos_ref[...].astype(f32) * freq_ref[...]
    cos = jnp.cos(ang)
    sin = jnp.sin(ang) * sign_ref[...]
    cos_ref[...] = cos
    sin_ref[...] = sin
    cost_ref[...] = cos.T
    sint_ref[...] = sin.T


def _rope_tables(positions, hd):
    s = positions.shape[-1]
    half = hd // 2
    inv_freq = ROPE_THETA ** (-jnp.arange(half, dtype=f32) / half)
    freq = jnp.concatenate([inv_freq, inv_freq]).reshape(1, hd)
    sign = jnp.concatenate([-jnp.ones((half,), f32), jnp.ones((half,), f32)]).reshape(1, hd)
    tm = ROW_TILE
    row = pl.BlockSpec((tm, hd), lambda i: (i, 0))
    col = pl.BlockSpec((hd, tm), lambda i: (0, i))
    return pl.pallas_call(
        _rope_table_kernel,
        out_shape=(jax.ShapeDtypeStruct((s, hd), f32), jax.ShapeDtypeStruct((s, hd), f32),
                   jax.ShapeDtypeStruct((hd, s), f32), jax.ShapeDtypeStruct((hd, s), f32)),
        grid=(s // tm,),
        in_specs=[
            pl.BlockSpec((tm, 1), lambda i: (i, 0)),
            pl.BlockSpec((1, hd), lambda i: (0, 0)),
            pl.BlockSpec((1, hd), lambda i: (0, 0)),
        ],
        out_specs=(row, row, col, col),
        compiler_params=_params("arbitrary"),
        name="rope_tables",
    )(positions.reshape(s, 1), freq, sign)


def _gla_in_kernel(x_ref, mod_ref, ng_ref, w_ref, wglr_ref, wup_ref, bg_ref,
                   q_ref, k_ref, v_ref, r_ref, g_ref, *, hdk, hdv, rank):
    h = _norm_mod(x_ref[...], ng_ref[...], mod_ref[1:2, :], mod_ref[0:1, :]).astype(bf16)
    glr3 = _dot(h, wglr_ref[...])
    hi = glr3.astype(bf16).astype(f32)
    lane = lax.broadcasted_iota(jnp.int32, glr3.shape, 1)
    use_lo = jnp.logical_and(lane >= rank, lane < 2 * rank)
    z = _dot(jnp.where(use_lo, glr3 - hi, hi).astype(bf16), wup_ref[...]) + bg_ref[...]
    log_sig = jnp.minimum(z, 0.0) - jnp.log(1.0 + jnp.exp(-jnp.abs(z)))
    g_ref[...] = log_sig / GLA_GATE_NORM
    q_ref[...] = _dot(h, w_ref[:, 0:hdk].astype(bf16))
    k_ref[...] = _dot(h, w_ref[:, hdk:2 * hdk].astype(bf16))
    v_ref[...] = _dot(h, w_ref[:, 2 * hdk:2 * hdk + hdv].astype(bf16))
    r_ref[...] = _dot(h, w_ref[:, 2 * hdk + hdv:2 * hdk + 2 * hdv].astype(bf16))


def _gla_in(x, mod, ng, w_in_all, layer, w_gate_up, b_gate):
    s, d = x.shape
    w_in = w_in_all[layer]
    hdk = w_gate_up.shape[1]
    rank = w_gate_up.shape[0]
    hdv = (w_in.shape[1] - 2 * hdk - rank) // 2
    n_main = 2 * hdk + 2 * hdv
    assert 3 * rank <= LANES
    w_glr = jnp.pad(jnp.tile(w_in[:, n_main:], (1, 3)), ((0, 0), (0, LANES - 3 * rank))).astype(bf16)
    wup_hi = w_gate_up.astype(bf16)
    wup_lo = (w_gate_up - wup_hi.astype(f32)).astype(bf16)
    wup = jnp.pad(jnp.concatenate([wup_hi, wup_hi, wup_lo], axis=0), ((0, LANES - 3 * rank), (0, 0)))
    tm = ROW_TILE
    row = lambda n: pl.BlockSpec((tm, n), lambda i: (i, 0))
    return pl.pallas_call(
        functools.partial(_gla_in_kernel, hdk=hdk, hdv=hdv, rank=rank),
        out_shape=(jax.ShapeDtypeStruct((s, hdk), f32), jax.ShapeDtypeStruct((s, hdk), f32),
                   jax.ShapeDtypeStruct((s, hdv), f32), jax.ShapeDtypeStruct((s, hdv), f32),
                   jax.ShapeDtypeStruct((s, hdk), f32)),
        grid=(s // tm,),
        in_specs=[row(d), _resident(mod.shape), _resident(ng.shape), _resident_layer(w_in_all.shape, layer),
                  _resident(w_glr.shape), _resident(wup.shape), _resident((1, hdk))],
        out_specs=(row(hdk), row(hdk), row(hdv), row(hdv), row(hdk)),
        compiler_params=_params("arbitrary"),
        name="gla_in_proj",
    )(x, mod, ng, w_in_all, w_glr, wup, b_gate.reshape(1, hdk))


def _gla_core_kernel(q_ref, k_ref, g_ref, v_ref, r_ref, og_ref, o_ref, s_ref,
                     qd_ref, ki_ref, kt_ref, gt_ref, *, heads):
    i = pl.program_id(0)

    @pl.when(i == 0)
    def _():
        s_ref[...] = jnp.zeros_like(s_ref)
        qd_ref[1] = jnp.zeros(qd_ref.shape[1:], f32)
        ki_ref[1] = jnp.zeros(ki_ref.shape[1:], bf16)
        kt_ref[1] = jnp.zeros(kt_ref.shape[1:], f32)
        gt_ref[1] = jnp.zeros(gt_ref.shape[1:], f32)

    t = q_ref.shape[0]
    dk = q_ref.shape[1] // heads
    dv = v_ref.shape[1] // heads
    lc = GLA_CHUNK
    nch = t // lc
    scale = dk ** -0.5

    def step(cur, prev):
        row = lax.broadcasted_iota(jnp.int32, (t, t), 0)
        col = lax.broadcasted_iota(jnp.int32, (t, t), 1)
        tri = jnp.logical_and(col <= row, (row // lc) == (col // lc))
        tri_b = jnp.where(tri, 1.0, 0.0).astype(bf16)
        lane_chunk = lax.broadcasted_iota(jnp.int32, (dk, t), 1) // lc
        row_chunk = lax.broadcasted_iota(jnp.int32, (t, dk), 0) // lc

        g_hi, g_lo = _split2(g_ref[...])
        gc = _dot(tri_b, g_hi) + _dot(tri_b, g_lo)
        g_last = jnp.concatenate(
            [jnp.broadcast_to(gc[c * lc + lc - 1:c * lc + lc, :], (lc, heads * dk)) for c in range(nch)], axis=0)
        k_all = k_ref[...]
        qd_ref[cur] = q_ref[...] * scale * jnp.exp(gc)
        ki_ref[cur] = (k_all * jnp.exp(-gc)).astype(bf16)
        kt_ref[cur] = (k_all * jnp.exp(g_last - gc)).T
        gt_ref[cur] = gc.T

        for h in range(heads):
            ks = slice(h * dk, (h + 1) * dk)
            vs = slice(h * dv, (h + 1) * dv)
            q_dec = qd_ref[prev, :, ks]
            k_tail_t = kt_ref[prev, ks, :]
            vh = v_ref[:, vs].astype(bf16)
            a = jnp.where(tri, _dot_nt(q_dec.astype(bf16), ki_ref[prev, :, ks]), 0.0).astype(bf16)
            kt_stack = jnp.concatenate(
                [jnp.where(lane_chunk == c, k_tail_t, 0.0).astype(bf16) for c in range(nch)], axis=0)
            u_all = _dot(kt_stack, vh)
            state = s_ref[h]
            states = []
            for c in range(nch):
                states.append(state.astype(bf16))
                decay = jnp.exp(gt_ref[prev, ks, c * lc + lc - 1:c * lc + lc])
                state = decay * state + u_all[c * dk:(c + 1) * dk, :]
            s_ref[h] = state
            q_blocks = [jnp.where(row_chunk == c, q_dec, 0.0).astype(bf16) for c in range(nch)]
            o = _dot(jnp.concatenate([a] + q_blocks, axis=1), jnp.concatenate([vh] + states, axis=0))
            ms = jnp.mean(o * o, axis=-1, keepdims=True)
            on = o * lax.rsqrt(ms + EPS) * og_ref[...]
            o_ref[:, vs] = (on * _silu(r_ref[:, vs])).astype(bf16)

    @pl.when(i % 2 == 0)
    def _():
        step(0, 1)

    @pl.when(i % 2 == 1)
    def _():
        step(1, 0)


def _gla_core(q, k, g, v, r, onorm_g):
    s, hdk = q.shape
    hdv = v.shape[1]
    heads = GLA_HEADS
    t = GLA_STEP
    n = s // t
    cur = lambda w: pl.BlockSpec((t, w), lambda i: (jnp.minimum(i, n - 1), 0))
    prev = lambda w: pl.BlockSpec((t, w), lambda i: (jnp.maximum(i - 1, 0), 0))
    return pl.pallas_call(
        functools.partial(_gla_core_kernel, heads=heads),
        out_shape=jax.ShapeDtypeStruct((s, hdv), bf16),
        grid=(n + 1,),
        in_specs=[cur(hdk), cur(hdk), cur(hdk), prev(hdv), prev(hdv), _resident((1, hdv // heads))],
        out_specs=prev(hdv),
        scratch_shapes=[pltpu.VMEM((heads, hdk // heads, hdv // heads), f32),
                        pltpu.VMEM((2, t, hdk), f32), pltpu.VMEM((2, t, hdk), bf16),
                        pltpu.VMEM((2, hdk, t), f32), pltpu.VMEM((2, hdk, t), f32)],
        compiler_params=_params("arbitrary"),
        name="gla_core",
    )(q, k, g, v, r, onorm_g.reshape(1, hdv // heads))


def _moba_in_kernel(x_ref, mod_ref, ng_ref, wqt_ref, wk_ref, wvt_ref, qg_ref, kg_ref, cos_ref, sin_ref,
                    cost_ref, sint_ref, qt_ref, k_ref, vt_ref, bias_ref, yq_ref, yk_ref, kmean_ref, *, heads):
    i = pl.program_id(0)
    tm = x_ref.shape[0]
    d_attn = qt_ref.shape[0]
    hd = d_attn // heads
    half = hd // 2

    @pl.when(i == 0)
    def _():
        yq_ref[1] = jnp.zeros(yq_ref.shape[1:], f32)
        yk_ref[1] = jnp.zeros(yk_ref.shape[1:], f32)
        kmean_ref[...] = jnp.zeros_like(kmean_ref)

    def step(cur, prev):
        hf = _norm_mod(x_ref[...], ng_ref[...], mod_ref[1:2, :], mod_ref[0:1, :])
        h = hf.astype(bf16)
        ht = hf.T.astype(bf16)
        yq_ref[cur] = _dot(wqt_ref[...], ht)
        yk_ref[cur] = _dot(h, wk_ref[...])
        vt_ref[...] = _dot(wvt_ref[...], ht).astype(bf16)

        cos_t = cost_ref[0:half, :]
        sin_t = sint_ref[half:hd, :]
        for hh in range(heads):
            y = yq_ref[prev, hh * hd:(hh + 1) * hd, :]
            yn = y * lax.rsqrt(jnp.mean(y * y, axis=0, keepdims=True) + EPS) * qg_ref[...]
            t1, t2 = yn[0:half, :], yn[half:hd, :]
            qt_ref[hh * hd:hh * hd + half, :] = t1 * cos_t - t2 * sin_t
            qt_ref[hh * hd + half:(hh + 1) * hd, :] = t2 * cos_t + t1 * sin_t

        blk0 = jnp.maximum(i - 1, 0) * (tm // MOBA_BLOCK)
        row_blk = blk0 + lax.broadcasted_iota(jnp.int32, (tm, hd), 0) // MOBA_BLOCK
        col = lax.broadcasted_iota(jnp.int32, (tm, hd), 1)
        extra = jnp.where(col == row_blk, 1.0, 0.0).astype(bf16)
        cos = cos_ref[...]
        sin = sin_ref[...]
        for hh in range(heads):
            hs = slice(hh * hd, (hh + 1) * hd)
            y = yk_ref[prev, :, hs]
            yn = y * lax.rsqrt(jnp.mean(y * y, axis=-1, keepdims=True) + EPS) * kg_ref[...]
            kr = yn * cos + pltpu.roll(yn, half, 1) * sin
            k_ref[:, 2 * hh * hd:(2 * hh + 1) * hd] = kr.astype(bf16)
            k_ref[:, (2 * hh + 1) * hd:(2 * hh + 2) * hd] = extra
            grp = pl.multiple_of((blk0 // SUBLANES) * SUBLANES, SUBLANES)
            rows = kmean_ref[pl.ds(grp, SUBLANES), hs]
            rid = grp + lax.broadcasted_iota(jnp.int32, rows.shape, 0)
            for b in range(tm // MOBA_BLOCK):
                mean_b = jnp.mean(kr[b * MOBA_BLOCK:(b + 1) * MOBA_BLOCK, :], axis=0, keepdims=True)
                rows = jnp.where(rid == blk0 + b, mean_b, rows)
            kmean_ref[pl.ds(grp, SUBLANES), hs] = rows

        nb = kmean_ref.shape[0]
        own_blk = blk0 + lax.broadcasted_iota(jnp.int32, (1, tm), 1) // MOBA_BLOCK
        blk = lax.broadcasted_iota(jnp.int32, (nb, tm), 0)
        for hh in range(heads):
            hs = slice(hh * hd, (hh + 1) * hd)
            bias = _moba_gate_bias(kmean_ref.at[:, hs], qt_ref[hs, :], own_blk, MOBA_MASK_VALUE)
            bias_ref[hh] = jnp.where(blk == own_blk, 0.0, bias).astype(bf16)

    @pl.when(i % 2 == 0)
    def _():
        step(0, 1)

    @pl.when(i % 2 == 1)
    def _():
        step(1, 0)


def _moba_in(x, mod, ng, w_in, qg, kg, tables):
    s, d = x.shape
    heads = MOBA_HEADS
    d_attn = w_in.shape[1] // 3
    hd = d_attn // heads
    tm = ROW_TILE
    nb = s // MOBA_BLOCK
    n = s // tm
    assert nb <= hd and nb % SUBLANES == 0 and SUBLANES % (tm // MOBA_BLOCK) == 0
    cos_t, sin_t, cos_tt, sin_tt = tables
    cur = lambda i: jnp.minimum(i, n - 1)
    prev = lambda i: jnp.maximum(i - 1, 0)
    wqt = w_in[:, 0:d_attn].T.astype(bf16)
    wk = w_in[:, d_attn:2 * d_attn].astype(bf16)
    wvt = w_in[:, 2 * d_attn:].T.astype(bf16)
    return pl.pallas_call(
        functools.partial(_moba_in_kernel, heads=heads),
        out_shape=(jax.ShapeDtypeStruct((d_attn, s), f32), jax.ShapeDtypeStruct((s, 2 * d_attn), bf16),
                   jax.ShapeDtypeStruct((d_attn, s), bf16), jax.ShapeDtypeStruct((heads, nb, s), bf16)),
        grid=(n + 1,),
        in_specs=[pl.BlockSpec((tm, d), lambda i: (cur(i), 0)), _resident(mod.shape), _resident(ng.shape),
                  _resident(wqt.shape), _resident(wk.shape), _resident(wvt.shape), _resident((hd, 1)),
                  _resident((1, hd)),
                  pl.BlockSpec((tm, hd), lambda i: (prev(i), 0)), pl.BlockSpec((tm, hd), lambda i: (prev(i), 0)),
                  pl.BlockSpec((hd, tm), lambda i: (0, prev(i))), pl.BlockSpec((hd, tm), lambda i: (0, prev(i)))],
        out_specs=(pl.BlockSpec((d_attn, tm), lambda i: (0, prev(i))),
                   pl.BlockSpec((tm, 2 * d_attn), lambda i: (prev(i), 0)),
                   pl.BlockSpec((d_attn, tm), lambda i: (0, cur(i))),
                   pl.BlockSpec((heads, nb, tm), lambda i: (0, 0, prev(i)))),
        scratch_shapes=[pltpu.VMEM((2, d_attn, tm), f32), pltpu.VMEM((2, tm, d_attn), f32),
                        pltpu.VMEM((nb, d_attn), f32)],
        compiler_params=_params("arbitrary"),
        name="moba_in_proj",
    )(x, mod, ng, wqt, wk, wvt, qg.reshape(hd, 1), kg.reshape(1, hd), cos_t, sin_t, cos_tt, sin_tt)


def _moba_gate_bias(kmean_ref, qt, own_blk, off_value):
    nb = kmean_ref.shape[0]
    tq = qt.shape[1]
    neg = jnp.float32(-jnp.inf)
    km = kmean_ref[...]
    km_hi = km.astype(bf16)
    km_lo = (km - km_hi.astype(f32)).astype(bf16)
    qt_hi = qt.astype(bf16)
    qt_lo = (qt - qt_hi.astype(f32)).astype(bf16)
    gate = _dot(km_hi, qt_hi) + _dot(km_lo, qt_hi) + _dot(km_hi, qt_lo)
    blk = lax.broadcasted_iota(jnp.int32, (nb, tq), 0)
    gate = jnp.where(blk < own_blk, gate, neg)
    bias = jnp.full((nb, tq), off_value, f32)
    for _ in range(MOBA_TOPK):
        top = jnp.max(gate, axis=0, keepdims=True)
        first = jnp.min(jnp.where(gate == top, blk, nb), axis=0, keepdims=True)
        first = jnp.where(top > neg, first, nb)
        pick = blk == first
        bias = jnp.where(pick, 0.0, bias)
        gate = jnp.where(pick, neg, gate)
    return bias


def _causal_own_scores(k_ref, qb, blk):
    bs = MOBA_BLOCK
    hd = qb.shape[0]
    st = _dot(k_ref[pl.ds(pl.multiple_of(blk * bs, bs), bs), 0:hd], qb)
    key_pos = lax.broadcasted_iota(jnp.int32, (bs, bs), 0)
    qry_pos = lax.broadcasted_iota(jnp.int32, (bs, bs), 1)
    return jnp.where(key_pos <= qry_pos, st, jnp.float32(-jnp.inf))


def _moba_attn_fast_kernel(qt_ref, k_ref, vt_ref, bias_ref, o_ref, acc_ref, l_ref, qaug_ref):
    hd, tq = qt_ref.shape
    bs = MOBA_BLOCK
    nt = tq // bs
    nb = bias_ref.shape[0]
    first = pl.program_id(1) * nt
    qaug_ref[0:hd, :] = (qt_ref[...] * (hd ** -0.5 * LOG2E)).astype(bf16)
    qaug_ref[hd:hd + nb, :] = bias_ref[...]
    qaug_ref[hd + nb:2 * hd, :] = jnp.zeros((hd - nb, tq), bf16)
    acc_ref[...] = jnp.zeros_like(acc_ref)
    l_ref[...] = jnp.zeros_like(l_ref)

    def group(g, nblk, causal):
        gk = nblk * bs
        off = pl.multiple_of(g * gk, gk)
        sg = _dot(k_ref[pl.ds(off, gk), :], qaug_ref[...])
        if causal:
            key_pos = off + lax.broadcasted_iota(jnp.int32, (gk, tq), 0)
            qry_pos = first * bs + lax.broadcasted_iota(jnp.int32, (gk, tq), 1)
            sg = jnp.where(key_pos <= qry_pos, sg, jnp.float32(-jnp.inf))
        pg = jnp.exp2(sg)
        l_ref[...] += jnp.sum(pg, axis=0, keepdims=True)
        acc_ref[...] += _dot(vt_ref[:, pl.ds(off, gk)], pg.astype(bf16))

    def past_group(g, carry):
        group(g, MOBA_GROUP, False)
        return carry

    def own_group(g, carry):
        group(g, nt, True)
        return carry

    n_past = first // MOBA_GROUP
    lax.fori_loop(0, n_past, past_group, 0)
    lax.fori_loop(n_past * (MOBA_GROUP // nt), first // nt + 1, own_group, 0)
    o_ref[...] = (acc_ref[...] / l_ref[...]).T.astype(bf16)


def _moba_attn_general_kernel(qt_ref, k_ref, vt_ref, sel_ref, o_ref, acc_ref, l_ref, m_ref, bias_ref):
    hd, tq = qt_ref.shape
    bs = MOBA_BLOCK
    nt = tq // bs
    for a in range(nt):
        ls = slice(a * bs, (a + 1) * bs)
        i = pl.program_id(1) * nt + a
        bias_ref[...] = jnp.where(sel_ref[:, ls].astype(f32) == 0.0, 0.0, -jnp.inf)
        qb = (qt_ref[:, ls] * (hd ** -0.5 * LOG2E)).astype(bf16)
        st = _causal_own_scores(k_ref, qb, i)
        m0 = jnp.max(st, axis=0, keepdims=True)
        p = jnp.exp2(st - m0)
        m_ref[...] = m0
        l_ref[:, ls] = jnp.sum(p, axis=0, keepdims=True)
        own = pl.multiple_of(i * bs, bs)
        acc_ref[:, ls] = _dot(vt_ref[:, pl.ds(own, bs)], p.astype(bf16))

        def past_block(j, carry, ls=ls, qb=qb):
            off = pl.multiple_of(j * bs, bs)
            sj = _dot(k_ref[pl.ds(off, bs), 0:hd], qb) + bias_ref[pl.ds(j, 1), :]
            m_old = m_ref[...]
            m_new = jnp.maximum(m_old, jnp.max(sj, axis=0, keepdims=True))
            alpha = jnp.exp2(m_old - m_new)
            pj = jnp.exp2(sj - m_new)
            l_ref[:, ls] = alpha * l_ref[:, ls] + jnp.sum(pj, axis=0, keepdims=True)
            acc_ref[:, ls] = alpha * acc_ref[:, ls] + _dot(vt_ref[:, pl.ds(off, bs)], pj.astype(bf16))
            m_ref[...] = m_new
            return carry

        lax.fori_loop(0, i, past_block, 0)
        o_ref[ls, :] = (acc_ref[:, ls] / l_ref[:, ls]).T.astype(bf16)


def _moba_attn(qt, k, vt, bias, qg, kg):
    d_attn, s = qt.shape
    heads = MOBA_HEADS
    hd = d_attn // heads
    tq = MOBA_QTILE
    bs = MOBA_BLOCK
    nb = s // bs
    assert nb % MOBA_GROUP == 0 and s % tq == 0 and tq % bs == 0 and MOBA_GROUP % (tq // bs) == 0

    def call(body, scratch):
        return pl.pallas_call(
            body,
            out_shape=jax.ShapeDtypeStruct((s, d_attn), bf16),
            grid=(heads, s // tq),
            in_specs=[
                pl.BlockSpec((hd, tq), lambda h, t: (h, t)),
                pl.BlockSpec((s, 2 * hd), lambda h, t: (0, h)),
                pl.BlockSpec((hd, s), lambda h, t: (h, 0)),
                pl.BlockSpec((None, nb, tq), lambda h, t: (h, 0, t)),
            ],
            out_specs=pl.BlockSpec((tq, hd), lambda h, t: (t, h)),
            scratch_shapes=scratch,
            compiler_params=_params("arbitrary", "arbitrary"),
            name=body.__name__.strip("_"),
        )(qt, k, vt, bias)

    stats = [pltpu.VMEM((hd, tq), f32), pltpu.VMEM((1, tq), f32)]
    fast = lambda: call(_moba_attn_fast_kernel, stats + [pltpu.VMEM((2 * hd, tq), bf16)])
    general = lambda: call(_moba_attn_general_kernel,
                           stats + [pltpu.VMEM((1, bs), f32), pltpu.VMEM((nb, bs), f32)])
    bound = 1.02 * hd ** 0.5 * LOG2E * jnp.max(jnp.abs(qg)) * jnp.max(jnp.abs(kg))
    return lax.cond(bound <= MOBA_FAST_RANGE, fast, general)


def _out_mlp_kernel(a_ref, x_ref, mod_ref, ng_ref, wo_ref, w1_ref, w2_ref, o_ref, acc_ref):
    x1 = x_ref[...] + mod_ref[2:3, :] * _dot(a_ref[...], wo_ref[...].astype(bf16))
    h = _norm_mod(x1, ng_ref[...], mod_ref[4:5, :], mod_ref[3:4, :]).astype(bf16)
    d_ff = w1_ref.shape[1]
    for c in range(d_ff // FF_CHUNK):
        cs = slice(c * FF_CHUNK, (c + 1) * FF_CHUNK)
        u = jnp.maximum(_dot(h, w1_ref[:, cs].astype(bf16)), 0.0)
        part = _dot((u * u).astype(bf16), w2_ref[cs, :].astype(bf16))
        if c == 0:
            acc_ref[...] = part
        else:
            acc_ref[...] += part
    o_ref[...] = x1 + mod_ref[5:6, :] * acc_ref[...]


def _out_mlp(a, x, mod, ng, w_out_all, mixer_layer, w1_all, w2_all, layer):
    s, d = x.shape
    tm = ROW_TILE
    row = lambda n: pl.BlockSpec((tm, n), lambda i: (i, 0))
    return pl.pallas_call(
        _out_mlp_kernel,
        out_shape=jax.ShapeDtypeStruct((s, d), f32),
        grid=(s // tm,),
        in_specs=[row(a.shape[1]), row(d), _resident(mod.shape), _resident(ng.shape),
                  _resident_layer(w_out_all.shape, mixer_layer), _resident_layer(w1_all.shape, layer),
                  _resident_layer(w2_all.shape, layer)],
        out_specs=row(d),
        scratch_shapes=[pltpu.VMEM((tm, d), f32)],
        compiler_params=_params("arbitrary"),
        name="out_proj_mlp",
    )(a, x, mod, ng, w_out_all, w1_all, w2_all)


def kernel(x, c, positions, ada_w, ada_b, norm_mix_g, norm_mlp_g, gla_w_in, gla_w_gate_up, gla_b_gate,
           gla_onorm_g, gla_w_out, moba_w_in, moba_q_norm_g, moba_k_norm_g, moba_w_out, mlp_w1, mlp_w2):
    b, s, d = x.shape
    assert b == 1 and s % ROW_TILE == 0 and s % MOBA_BLOCK == 0
    depth = ada_w.shape[0]
    xs = x.reshape(s, d)
    mod = _ada_mod(c, ada_w, ada_b)
    moba_hd = moba_w_out.shape[1] // MOBA_HEADS
    tables = _rope_tables(positions, moba_hd)
    for i in range(depth):
        j = i // 2
        ng = norm_mix_g[i].reshape(1, d)
        if i % 2 == 0:
            q, k, v, r, g = _gla_in(xs, mod[i], ng, gla_w_in, j, gla_w_gate_up[j], gla_b_gate[j])
            a = _gla_core(q, k, g, v, r, gla_onorm_g[j])
            w_out = gla_w_out
        else:
            q, k, vt, bias = _moba_in(xs, mod[i], ng, moba_w_in[j], moba_q_norm_g[j], moba_k_norm_g[j], tables)
            a = _moba_attn(q, k, vt, bias, moba_q_norm_g[j], moba_k_norm_g[j])
            w_out = moba_w_out
        xs = _out_mlp(a, xs, mod[i], norm_mlp_g[i].reshape(1, d), w_out, j, mlp_w1, mlp_w2, i)
    return xs.reshape(b, s, d)
```

```python
import functools

import jax
import jax.numpy as jnp
from jax import lax
from jax.experimental import pallas as pl
from jax.experimental.pallas import tpu as pltpu

f32 = jnp.float32
bf16 = jnp.bfloat16

EPS = 1e-6
GLA_HEADS = 4
GLA_GATE_NORM = 16.0
GLA_CHUNK = 64
MOBA_HEADS = 8
MOBA_BLOCK = 256
MOBA_TOPK = 3
MOBA_GROUP = 8
MOBA_QTILE = 1024
LOG2E = 1.4426950408889634
MOBA_MASK_VALUE = -2.0 ** 30
MOBA_FAST_RANGE = 32.0
ROPE_THETA = 10000.0

LANES = 128
SUBLANES = 8
VMEM_LIMIT = 56 * 1024 * 1024

ROW_TILE = 512
GLA_STEP = 256
FF_CHUNK = 1024


def _params(*sem):
    return pltpu.CompilerParams(dimension_semantics=sem, vmem_limit_bytes=VMEM_LIMIT)


def _resident(shape):
    nd = len(shape)
    return pl.BlockSpec(shape, lambda *_: (0,) * nd, pipeline_mode=pl.Buffered(1))


def _resident_layer(stacked_shape, layer):
    nd = len(stacked_shape) - 1
    return pl.BlockSpec((None,) + tuple(stacked_shape[1:]), lambda *_: (layer,) + (0,) * nd,
                        pipeline_mode=pl.Buffered(1))


def _dot(a, b):
    return jnp.dot(a, b, preferred_element_type=f32)


def _dot_nt(a, b):
    return lax.dot_general(a, b, (((1,), (1,)), ((), ())), preferred_element_type=f32)


def _split2(x):
    hi = x.astype(bf16)
    return hi, (x - hi.astype(f32)).astype(bf16)


def _silu(x):
    return x / (1.0 + jnp.exp(-x))


def _norm_mod(x, g, scale, shift):
    ms = jnp.mean(x * x, axis=-1, keepdims=True)
    y = x * lax.rsqrt(ms + EPS)
    return (y * g) * (1.0 + scale) + shift


def _ada_kernel(c_ref, w_ref, b_ref, o_ref):
    ca = _silu(c_ref[...])
    o_ref[...] = jnp.sum(ca * w_ref[...], axis=0, keepdims=True) + b_ref[...]


def _ada_mod(c, ada_w, ada_b):
    depth, d, n = ada_w.shape
    tn = 1536
    out = pl.pallas_call(
        _ada_kernel,
        out_shape=jax.ShapeDtypeStruct((depth, 1, n), f32),
        grid=(depth, n // tn),
        in_specs=[
            pl.BlockSpec((d, 1), lambda l, j: (0, 0)),
            pl.BlockSpec((None, d, tn), lambda l, j: (l, 0, j)),
            pl.BlockSpec((None, 1, tn), lambda l, j: (l, 0, j)),
        ],
        out_specs=pl.BlockSpec((None, 1, tn), lambda l, j: (l, 0, j)),
        compiler_params=_params("arbitrary", "arbitrary"),
        name="ada_mod",
    )(c.reshape(d, 1), ada_w, ada_b.reshape(depth, 1, n))
    return out.reshape(depth, 6, d)


def _rope_table_kernel(pos_ref, freq_ref, sign_ref, cos_ref, sin_ref, cost_ref, sint_ref):
    ang = pos_ref[...].astype(f32) * freq_ref[...]
    cos = jnp.cos(ang)
    sin = jnp.sin(ang) * sign_ref[...]
    cos_ref[...] = cos
    sin_ref[...] = sin
    cost_ref[...] = cos.T
    sint_ref[...] = sin.T


def _rope_tables(positions, hd):
    s = positions.shape[-1]
    half = hd // 2
    inv_freq = ROPE_THETA ** (-jnp.arange(half, dtype=f32) / half)
    freq = jnp.concatenate([inv_freq, inv_freq]).reshape(1, hd)
    sign = jnp.concatenate([-jnp.ones((half,), f32), jnp.ones((half,), f32)]).reshape(1, hd)
    tm = ROW_TILE
    row = pl.BlockSpec((tm, hd), lambda i: (i, 0))
    col = pl.BlockSpec((hd, tm), lambda i: (0, i))
    return pl.pallas_call(
        _rope_table_kernel,
        out_shape=(jax.ShapeDtypeStruct((s, hd), f32), jax.ShapeDtypeStruct((s, hd), f32),
                   jax.ShapeDtypeStruct((hd, s), f32), jax.ShapeDtypeStruct((hd, s), f32)),
        grid=(s // tm,),
        in_specs=[
            pl.BlockSpec((tm, 1), lambda i: (i, 0)),
            pl.BlockSpec((1, hd), lambda i: (0, 0)),
            pl.BlockSpec((1, hd), lambda i: (0, 0)),
        ],
        out_specs=(row, row, col, col),
        compiler_params=_params("arbitrary"),
        name="rope_tables",
    )(positions.reshape(s, 1), freq, sign)


def _gla_in_kernel(x_ref, mod_ref, ng_ref, w_ref, wglr_ref, wup_ref, bg_ref,
                   q_ref, k_ref, v_ref, r_ref, g_ref, *, hdk, hdv, rank):
    h = _norm_mod(x_ref[...], ng_ref[...], mod_ref[1:2, :], mod_ref[0:1, :]).astype(bf16)
    glr3 = _dot(h, wglr_ref[...])
    hi = glr3.astype(bf16).astype(f32)
    lane = lax.broadcasted_iota(jnp.int32, glr3.shape, 1)
    use_lo = jnp.logical_and(lane >= rank, lane < 2 * rank)
    z = _dot(jnp.where(use_lo, glr3 - hi, hi).astype(bf16), wup_ref[...]) + bg_ref[...]
    log_sig = jnp.minimum(z, 0.0) - jnp.log(1.0 + jnp.exp(-jnp.abs(z)))
    g_ref[...] = log_sig / GLA_GATE_NORM
    q_ref[...] = _dot(h, w_ref[:, 0:hdk].astype(bf16))
    k_ref[...] = _dot(h, w_ref[:, hdk:2 * hdk].astype(bf16))
    v_ref[...] = _dot(h, w_ref[:, 2 * hdk:2 * hdk + hdv].astype(bf16))
    r_ref[...] = _dot(h, w_ref[:, 2 * hdk + hdv:2 * hdk + 2 * hdv].astype(bf16))


def _gla_in(x, mod, ng, w_in_all, layer, w_gate_up, b_gate):
    s, d = x.shape
    w_in = w_in_all[layer]
    hdk = w_gate_up.shape[1]
    rank = w_gate_up.shape[0]
    hdv = (w_in.shape[1] - 2 * hdk - rank) // 2
    n_main = 2 * hdk + 2 * hdv
    assert 3 * rank <= LANES
    w_glr = jnp.pad(jnp.tile(w_in[:, n_main:], (1, 3)), ((0, 0), (0, LANES - 3 * rank))).astype(bf16)
    wup_hi = w_gate_up.astype(bf16)
    wup_lo = (w_gate_up - wup_hi.astype(f32)).astype(bf16)
    wup = jnp.pad(jnp.concatenate([wup_hi, wup_hi, wup_lo], axis=0), ((0, LANES - 3 * rank), (0, 0)))
    tm = ROW_TILE
    row = lambda n: pl.BlockSpec((tm, n), lambda i: (i, 0))
    return pl.pallas_call(
        functools.partial(_gla_in_kernel, hdk=hdk, hdv=hdv, rank=rank),
        out_shape=(jax.ShapeDtypeStruct((s, hdk), f32), jax.ShapeDtypeStruct((s, hdk), f32),
                   jax.ShapeDtypeStruct((s, hdv), f32), jax.ShapeDtypeStruct((s, hdv), f32),
                   jax.ShapeDtypeStruct((s, hdk), f32)),
        grid=(s // tm,),
        in_specs=[row(d), _resident(mod.shape), _resident(ng.shape), _resident_layer(w_in_all.shape, layer),
                  _resident(w_glr.shape), _resident(wup.shape), _resident((1, hdk))],
        out_specs=(row(hdk), row(hdk), row(hdv), row(hdv), row(hdk)),
        compiler_params=_params("arbitrary"),
        name="gla_in_proj",
    )(x, mod, ng, w_in_all, w_glr, wup, b_gate.reshape(1, hdk))


def _gla_core_kernel(q_ref, k_ref, g_ref, v_ref, r_ref, og_ref, o_ref, s_ref,
                     qd_ref, ki_ref, kt_ref, gt_ref, *, heads):
    i = pl.program_id(0)

    @pl.when(i == 0)
    def _():
        s_ref[...] = jnp.zeros_like(s_ref)
        qd_ref[1] = jnp.zeros(qd_ref.shape[1:], f32)
        ki_ref[1] = jnp.zeros(ki_ref.shape[1:], bf16)
        kt_ref[1] = jnp.zeros(kt_ref.shape[1:], f32)
        gt_ref[1] = jnp.zeros(gt_ref.shape[1:], f32)

    t = q_ref.shape[0]
    dk = q_ref.shape[1] // heads
    dv = v_ref.shape[1] // heads
    lc = GLA_CHUNK
    nch = t // lc
    scale = dk ** -0.5

    def step(cur, prev):
        row = lax.broadcasted_iota(jnp.int32, (t, t), 0)
        col = lax.broadcasted_iota(jnp.int32, (t, t), 1)
        tri = jnp.logical_and(col <= row, (row // lc) == (col // lc))
        tri_b = jnp.where(tri, 1.0, 0.0).astype(bf16)
        lane_chunk = lax.broadcasted_iota(jnp.int32, (dk, t), 1) // lc
        row_chunk = lax.broadcasted_iota(jnp.int32, (t, dk), 0) // lc

        g_hi, g_lo = _split2(g_ref[...])
        gc = _dot(tri_b, g_hi) + _dot(tri_b, g_lo)
        g_last = jnp.concatenate(
            [jnp.broadcast_to(gc[c * lc + lc - 1:c * lc + lc, :], (lc, heads * dk)) for c in range(nch)], axis=0)
        k_all = k_ref[...]
        qd_ref[cur] = q_ref[...] * scale * jnp.exp(gc)
        ki_ref[cur] = (k_all * jnp.exp(-gc)).astype(bf16)
        kt_ref[cur] = (k_all * jnp.exp(g_last - gc)).T
        gt_ref[cur] = gc.T

        for h in range(heads):
            ks = slice(h * dk, (h + 1) * dk)
            vs = slice(h * dv, (h + 1) * dv)
            q_dec = qd_ref[prev, :, ks]
            k_tail_t = kt_ref[prev, ks, :]
            vh = v_ref[:, vs].astype(bf16)
            a = jnp.where(tri, _dot_nt(q_dec.astype(bf16), ki_ref[prev, :, ks]), 0.0).astype(bf16)
            kt_stack = jnp.concatenate(
                [jnp.where(lane_chunk == c, k_tail_t, 0.0).astype(bf16) for c in range(nch)], axis=0)
            u_all = _dot(kt_stack, vh)
            state = s_ref[h]
            states = []
            for c in range(nch):
                states.append(state.astype(bf16))
                decay = jnp.exp(gt_ref[prev, ks, c * lc + lc - 1:c * lc + lc])
                state = decay * state + u_all[c * dk:(c + 1) * dk, :]
            s_ref[h] = state
            q_blocks = [jnp.where(row_chunk == c, q_dec, 0.0).astype(bf16) for c in range(nch)]
            o = _dot(jnp.concatenate([a] + q_blocks, axis=1), jnp.concatenate([vh] + states, axis=0))
            ms = jnp.mean(o * o, axis=-1, keepdims=True)
            on = o * lax.rsqrt(ms + EPS) * og_ref[...]
            o_ref[:, vs] = (on * _silu(r_ref[:, vs])).astype(bf16)

    @pl.when(i % 2 == 0)
    def _():
        step(0, 1)

    @pl.when(i % 2 == 1)
    def _():
        step(1, 0)


def _gla_core(q, k, g, v, r, onorm_g):
    s, hdk = q.shape
    hdv = v.shape[1]
    heads = GLA_HEADS
    t = GLA_STEP
    n = s // t
    cur = lambda w: pl.BlockSpec((t, w), lambda i: (jnp.minimum(i, n - 1), 0))
    prev = lambda w: pl.BlockSpec((t, w), lambda i: (jnp.maximum(i - 1, 0), 0))
    return pl.pallas_call(
        functools.partial(_gla_core_kernel, heads=heads),
        out_shape=jax.ShapeDtypeStruct((s, hdv), bf16),
        grid=(n + 1,),
        in_specs=[cur(hdk), cur(hdk), cur(hdk), prev(hdv), prev(hdv), _resident((1, hdv // heads))],
        out_specs=prev(hdv),
        scratch_shapes=[pltpu.VMEM((heads, hdk // heads, hdv // heads), f32),
                        pltpu.VMEM((2, t, hdk), f32), pltpu.VMEM((2, t, hdk), bf16),
                        pltpu.VMEM((2, hdk, t), f32), pltpu.VMEM((2, hdk, t), f32)],
        compiler_params=_params("arbitrary"),
        name="gla_core",
    )(q, k, g, v, r, onorm_g.reshape(1, hdv // heads))


def _moba_in_kernel(x_ref, mod_ref, ng_ref, wqt_ref, wk_ref, wvt_ref, qg_ref, kg_ref, cos_ref, sin_ref,
                    cost_ref, sint_ref, qt_ref, k_ref, vt_ref, bias_ref, yq_ref, yk_ref, kmean_ref, *, heads):
    i = pl.program_id(0)
    tm = x_ref.shape[0]
    d_attn = qt_ref.shape[0]
    hd = d_attn // heads
    half = hd // 2

    @pl.when(i == 0)
    def _():
        yq_ref[1] = jnp.zeros(yq_ref.shape[1:], f32)
        yk_ref[1] = jnp.zeros(yk_ref.shape[1:], f32)
        kmean_ref[...] = jnp.zeros_like(kmean_ref)

    def step(cur, prev):
        hf = _norm_mod(x_ref[...], ng_ref[...], mod_ref[1:2, :], mod_ref[0:1, :])
        h = hf.astype(bf16)
        ht = hf.T.astype(bf16)
        yq_ref[cur] = _dot(wqt_ref[...], ht)
        yk_ref[cur] = _dot(h, wk_ref[...])
        vt_ref[...] = _dot(wvt_ref[...], ht).astype(bf16)

        cos_t = cost_ref[0:half, :]
        sin_t = sint_ref[half:hd, :]
        for hh in range(heads):
            y = yq_ref[prev, hh * hd:(hh + 1) * hd, :]
            yn = y * lax.rsqrt(jnp.mean(y * y, axis=0, keepdims=True) + EPS) * qg_ref[...]
            t1, t2 = yn[0:half, :], yn[half:hd, :]
            qt_ref[hh * hd:hh * hd + half, :] = t1 * cos_t - t2 * sin_t
            qt_ref[hh * hd + half:(hh + 1) * hd, :] = t2 * cos_t + t1 * sin_t

        blk0 = jnp.maximum(i - 1, 0) * (tm // MOBA_BLOCK)
        row_blk = blk0 + lax.broadcasted_iota(jnp.int32, (tm, hd), 0) // MOBA_BLOCK
        col = lax.broadcasted_iota(jnp.int32, (tm, hd), 1)
        extra = jnp.where(col == row_blk, 1.0, 0.0).astype(bf16)
        cos = cos_ref[...]
        sin = sin_ref[...]
        for hh in range(heads):
            hs = slice(hh * hd, (hh + 1) * hd)
            y = yk_ref[prev, :, hs]
            yn = y * lax.rsqrt(jnp.mean(y * y, axis=-1, keepdims=True) + EPS) * kg_ref[...]
            kr = yn * cos + pltpu.roll(yn, half, 1) * sin
            k_ref[:, 2 * hh * hd:(2 * hh + 1) * hd] = kr.astype(bf16)
            k_ref[:, (2 * hh + 1) * hd:(2 * hh + 2) * hd] = extra
            grp = pl.multiple_of((blk0 // SUBLANES) * SUBLANES, SUBLANES)
            rows = kmean_ref[pl.ds(grp, SUBLANES), hs]
            rid = grp + lax.broadcasted_iota(jnp.int32, rows.shape, 0)
            for b in range(tm // MOBA_BLOCK):
                mean_b = jnp.mean(kr[b * MOBA_BLOCK:(b + 1) * MOBA_BLOCK, :], axis=0, keepdims=True)
                rows = jnp.where(rid == blk0 + b, mean_b, rows)
            kmean_ref[pl.ds(grp, SUBLANES), hs] = rows

        nb = kmean_ref.shape[0]
        own_blk = blk0 + lax.broadcasted_iota(jnp.int32, (1, tm), 1) // MOBA_BLOCK
        blk = lax.broadcasted_iota(jnp.int32, (nb, tm), 0)
        for hh in range(heads):
            hs = slice(hh * hd, (hh + 1) * hd)
            bias = _moba_gate_bias(kmean_ref.at[:, hs], qt_ref[hs, :], own_blk, MOBA_MASK_VALUE)
            bias_ref[hh] = jnp.where(blk == own_blk, 0.0, bias).astype(bf16)

    @pl.when(i % 2 == 0)
    def _():
        step(0, 1)

    @pl.when(i % 2 == 1)
    def _():
        step(1, 0)


def _moba_in(x, mod, ng, w_in, qg, kg, tables):
    s, d = x.shape
    heads = MOBA_HEADS
    d_attn = w_in.shape[1] // 3
    hd = d_attn // heads
    tm = ROW_TILE
    nb = s // MOBA_BLOCK
    n = s // tm
    assert nb <= hd and nb % SUBLANES == 0 and SUBLANES % (tm // MOBA_BLOCK) == 0
    cos_t, sin_t, cos_tt, sin_tt = tables
    cur = lambda i: jnp.minimum(i, n - 1)
    prev = lambda i: jnp.maximum(i - 1, 0)
    wqt = w_in[:, 0:d_attn].T.astype(bf16)
    wk = w_in[:, d_attn:2 * d_attn].astype(bf16)
    wvt = w_in[:, 2 * d_attn:].T.astype(bf16)
    return pl.pallas_call(
        functools.partial(_moba_in_kernel, heads=heads),
        out_shape=(jax.ShapeDtypeStruct((d_attn, s), f32), jax.ShapeDtypeStruct((s, 2 * d_attn), bf16),
                   jax.ShapeDtypeStruct((d_attn, s), bf16), jax.ShapeDtypeStruct((heads, nb, s), bf16)),
        grid=(n + 1,),
        in_specs=[pl.BlockSpec((tm, d), lambda i: (cur(i), 0)), _resident(mod.shape), _resident(ng.shape),
                  _resident(wqt.shape), _resident(wk.shape), _resident(wvt.shape), _resident((hd, 1)),
                  _resident((1, hd)),
                  pl.BlockSpec((tm, hd), lambda i: (prev(i), 0)), pl.BlockSpec((tm, hd), lambda i: (prev(i), 0)),
                  pl.BlockSpec((hd, tm), lambda i: (0, prev(i))), pl.BlockSpec((hd, tm), lambda i: (0, prev(i)))],
        out_specs=(pl.BlockSpec((d_attn, tm), lambda i: (0, prev(i))),
                   pl.BlockSpec((tm, 2 * d_attn), lambda i: (prev(i), 0)),
                   pl.BlockSpec((d_attn, tm), lambda i: (0, cur(i))),
                   pl.BlockSpec((heads, nb, tm), lambda i: (0, 0, prev(i)))),
        scratch_shapes=[pltpu.VMEM((2, d_attn, tm), f32), pltpu.VMEM((2, tm, d_attn), f32),
                        pltpu.VMEM((nb, d_attn), f32)],
        compiler_params=_params("arbitrary"),
        name="moba_in_proj",
    )(x, mod, ng, wqt, wk, wvt, qg.reshape(hd, 1), kg.reshape(1, hd), cos_t, sin_t, cos_tt, sin_tt)


def _moba_gate_bias(kmean_ref, qt, own_blk, off_value):
    nb = kmean_ref.shape[0]
    tq = qt.shape[1]
    neg = jnp.float32(-jnp.inf)
    km = kmean_ref[...]
    km_hi = km.astype(bf16)
    km_lo = (km - km_hi.astype(f32)).astype(bf16)
    qt_hi = qt.astype(bf16)
    qt_lo = (qt - qt_hi.astype(f32)).astype(bf16)
    gate = _dot(km_hi, qt_hi) + _dot(km_lo, qt_hi) + _dot(km_hi, qt_lo)
    blk = lax.broadcasted_iota(jnp.int32, (nb, tq), 0)
    gate = jnp.where(blk < own_blk, gate, neg)
    bias = jnp.full((nb, tq), off_value, f32)
    for _ in range(MOBA_TOPK):
        top = jnp.max(gate, axis=0, keepdims=True)
        first = jnp.min(jnp.where(gate == top, blk, nb), axis=0, keepdims=True)
        first = jnp.where(top > neg, first, nb)
        pick = blk == first
        bias = jnp.where(pick, 0.0, bias)
        gate = jnp.where(pick, neg, gate)
    return bias


def _causal_own_scores(k_ref, qb, blk):
    bs = MOBA_BLOCK
    hd = qb.shape[0]
    st = _dot(k_ref[pl.ds(pl.multiple_of(blk * bs, bs), bs), 0:hd], qb)
    key_pos = lax.broadcasted_iota(jnp.int32, (bs, bs), 0)
    qry_pos = lax.broadcasted_iota(jnp.int32, (bs, bs), 1)
    return jnp.where(key_pos <= qry_pos, st, jnp.float32(-jnp.inf))


def _moba_attn_fast_kernel(qt_ref, k_ref, vt_ref, bias_ref, o_ref, acc_ref, l_ref, qaug_ref):
    hd, tq = qt_ref.shape
    bs = MOBA_BLOCK
    nt = tq // bs
    nb = bias_ref.shape[0]
    first = pl.program_id(1) * nt
    qaug_ref[0:hd, :] = (qt_ref[...] * (hd ** -0.5 * LOG2E)).astype(bf16)
    qaug_ref[hd:hd + nb, :] = bias_ref[...]
    qaug_ref[hd + nb:2 * hd, :] = jnp.zeros((hd - nb, tq), bf16)
    acc_ref[...] = jnp.zeros_like(acc_ref)
    l_ref[...] = jnp.zeros_like(l_ref)

    def group(g, nblk, causal):
        gk = nblk * bs
        off = pl.multiple_of(g * gk, gk)
        sg = _dot(k_ref[pl.ds(off, gk), :], qaug_ref[...])
        if causal:
            key_pos = off + lax.broadcasted_iota(jnp.int32, (gk, tq), 0)
            qry_pos = first * bs + lax.broadcasted_iota(jnp.int32, (gk, tq), 1)
            sg = jnp.where(key_pos <= qry_pos, sg, jnp.float32(-jnp.inf))
        pg = jnp.exp2(sg)
        l_ref[...] += jnp.sum(pg, axis=0, keepdims=True)
        acc_ref[...] += _dot(vt_ref[:, pl.ds(off, gk)], pg.astype(bf16))

    def past_group(g, carry):
        group(g, MOBA_GROUP, False)
        return carry

    def own_group(g, carry):
        group(g, nt, True)
        return carry

    n_past = first // MOBA_GROUP
    lax.fori_loop(0, n_past, past_group, 0)
    fills_group = (first + nt) % MOBA_GROUP == 0

    @pl.when(fills_group)
    def _():
        group(n_past, MOBA_GROUP, True)

    @pl.when(jnp.logical_not(fills_group))
    def _():
        lax.fori_loop(n_past * (MOBA_GROUP // nt), first // nt + 1, own_group, 0)

    o_ref[...] = (acc_ref[...] / l_ref[...]).T.astype(bf16)


def _moba_attn_general_kernel(qt_ref, k_ref, vt_ref, sel_ref, o_ref, acc_ref, l_ref, m_ref, bias_ref):
    hd, tq = qt_ref.shape
    bs = MOBA_BLOCK
    nt = tq // bs
    for a in range(nt):
        ls = slice(a * bs, (a + 1) * bs)
        i = pl.program_id(1) * nt + a
        bias_ref[...] = jnp.where(sel_ref[:, ls].astype(f32) == 0.0, 0.0, -jnp.inf)
        qb = (qt_ref[:, ls] * (hd ** -0.5 * LOG2E)).astype(bf16)
        st = _causal_own_scores(k_ref, qb, i)
        m0 = jnp.max(st, axis=0, keepdims=True)
        p = jnp.exp2(st - m0)
        m_ref[...] = m0
        l_ref[:, ls] = jnp.sum(p, axis=0, keepdims=True)
        own = pl.multiple_of(i * bs, bs)
        acc_ref[:, ls] = _dot(vt_ref[:, pl.ds(own, bs)], p.astype(bf16))

        def past_block(j, carry, ls=ls, qb=qb):
            off = pl.multiple_of(j * bs, bs)
            sj = _dot(k_ref[pl.ds(off, bs), 0:hd], qb) + bias_ref[pl.ds(j, 1), :]
            m_old = m_ref[...]
            m_new = jnp.maximum(m_old, jnp.max(sj, axis=0, keepdims=True))
            alpha = jnp.exp2(m_old - m_new)
            pj = jnp.exp2(sj - m_new)
            l_ref[:, ls] = alpha * l_ref[:, ls] + jnp.sum(pj, axis=0, keepdims=True)
            acc_ref[:, ls] = alpha * acc_ref[:, ls] + _dot(vt_ref[:, pl.ds(off, bs)], pj.astype(bf16))
            m_ref[...] = m_new
            return carry

        lax.fori_loop(0, i, past_block, 0)
        o_ref[ls, :] = (acc_ref[:, ls] / l_ref[:, ls]).T.astype(bf16)


def _moba_attn(qt, k, vt, bias, qg, kg):
    d_attn, s = qt.shape
    heads = MOBA_HEADS
    hd = d_attn // heads
    tq = MOBA_QTILE
    bs = MOBA_BLOCK
    nb = s // bs
    assert nb % MOBA_GROUP == 0 and s % tq == 0 and tq % bs == 0 and MOBA_GROUP % (tq // bs) == 0

    def call(body, scratch):
        return pl.pallas_call(
            body,
            out_shape=jax.ShapeDtypeStruct((s, d_attn), bf16),
            grid=(heads, s // tq),
            in_specs=[
                pl.BlockSpec((hd, tq), lambda h, t: (h, t)),
                pl.BlockSpec((s, 2 * hd), lambda h, t: (0, h)),
                pl.BlockSpec((hd, s), lambda h, t: (h, 0)),
                pl.BlockSpec((None, nb, tq), lambda h, t: (h, 0, t)),
            ],
            out_specs=pl.BlockSpec((tq, hd), lambda h, t: (t, h)),
            scratch_shapes=scratch,
            compiler_params=_params("arbitrary", "arbitrary"),
            name=body.__name__.strip("_"),
        )(qt, k, vt, bias)

    stats = [pltpu.VMEM((hd, tq), f32), pltpu.VMEM((1, tq), f32)]
    fast = lambda: call(_moba_attn_fast_kernel, stats + [pltpu.VMEM((2 * hd, tq), bf16)])
    general = lambda: call(_moba_attn_general_kernel,
                           stats + [pltpu.VMEM((1, bs), f32), pltpu.VMEM((nb, bs), f32)])
    bound = 1.02 * hd ** 0.5 * LOG2E * jnp.max(jnp.abs(qg)) * jnp.max(jnp.abs(kg))
    return lax.cond(bound <= MOBA_FAST_RANGE, fast, general)


def _out_mlp_kernel(a_ref, x_ref, mod_ref, ng_ref, wo_ref, w1_ref, w2_ref, o_ref, acc_ref):
    x1 = x_ref[...] + mod_ref[2:3, :] * _dot(a_ref[...], wo_ref[...].astype(bf16))
    h = _norm_mod(x1, ng_ref[...], mod_ref[4:5, :], mod_ref[3:4, :]).astype(bf16)
    d_ff = w1_ref.shape[1]
    for c in range(d_ff // FF_CHUNK):
        cs = slice(c * FF_CHUNK, (c + 1) * FF_CHUNK)
        u = jnp.maximum(_dot(h, w1_ref[:, cs].astype(bf16)), 0.0)
        part = _dot((u * u).astype(bf16), w2_ref[cs, :].astype(bf16))
        if c == 0:
            acc_ref[...] = part
        else:
            acc_ref[...] += part
    o_ref[...] = x1 + mod_ref[5:6, :] * acc_ref[...]


def _out_mlp(a, x, mod, ng, w_out_all, mixer_layer, w1_all, w2_all, layer):
    s, d = x.shape
    tm = ROW_TILE
    row = lambda n: pl.BlockSpec((tm, n), lambda i: (i, 0))
    return pl.pallas_call(
        _out_mlp_kernel,
        out_shape=jax.ShapeDtypeStruct((s, d), f32),
        grid=(s // tm,),
        in_specs=[row(a.shape[1]), row(d), _resident(mod.shape), _resident(ng.shape),
                  _resident_layer(w_out_all.shape, mixer_layer), _resident_layer(w1_all.shape, layer),
                  _resident_layer(w2_all.shape, layer)],
        out_specs=row(d),
        scratch_shapes=[pltpu.VMEM((tm, d), f32)],
        compiler_params=_params("arbitrary"),
        name="out_proj_mlp",
    )(a, x, mod, ng, w_out_all, w1_all, w2_all)


def kernel(x, c, positions, ada_w, ada_b, norm_mix_g, norm_mlp_g, gla_w_in, gla_w_gate_up, gla_b_gate,
           gla_onorm_g, gla_w_out, moba_w_in, moba_q_norm_g, moba_k_norm_g, moba_w_out, mlp_w1, mlp_w2):
    b, s, d = x.shape
    assert b == 1 and s % ROW_TILE == 0 and s % MOBA_BLOCK == 0
    depth = ada_w.shape[0]
    xs = x.reshape(s, d)
    mod = _ada_mod(c, ada_w, ada_b)
    moba_hd = moba_w_out.shape[1] // MOBA_HEADS
    tables = _rope_tables(positions, moba_hd)
    for i in range(depth):
        j = i // 2
        ng = norm_mix_g[i].reshape(1, d)
        if i % 2 == 0:
            q, k, v, r, g = _gla_in(xs, mod[i], ng, gla_w_in, j, gla_w_gate_up[j], gla_b_gate[j])
            a = _gla_core(q, k, g, v, r, gla_onorm_g[j])
            w_out = gla_w_out
        else:
            q, k, vt, bias = _moba_in(xs, mod[i], ng, moba_w_in[j], moba_q_norm_g[j], moba_k_norm_g[j], tables)
            a = _moba_attn(q, k, vt, bias, moba_q_norm_g[j], moba_k_norm_g[j])
            w_out = moba_w_out
        xs = _out_mlp(a, xs, mod[i], norm_mlp_g[i].reshape(1, d), w_out, j, mlp_w1, mlp_w2, i)
    return xs.reshape(b, s, d)
```

```python
import functools

import jax
import jax.numpy as jnp
from jax import lax
from jax.experimental import pallas as pl
from jax.experimental.pallas import tpu as pltpu

f32 = jnp.float32
bf16 = jnp.bfloat16

EPS = 1e-6
GLA_HEADS = 4
GLA_GATE_NORM = 16.0
GLA_CHUNK = 64
MOBA_HEADS = 8
MOBA_BLOCK = 256
MOBA_TOPK = 3
MOBA_GROUP = 8
MOBA_QTILE = 1024
LOG2E = 1.4426950408889634
MOBA_MASK_VALUE = -2.0 ** 30
MOBA_FAST_RANGE = 32.0
ROPE_THETA = 10000.0

LANES = 128
SUBLANES = 8
VMEM_LIMIT = 56 * 1024 * 1024

ROW_TILE = 512
GLA_STEP = 256
FF_CHUNK = 1024


def _params(*sem):
    return pltpu.CompilerParams(dimension_semantics=sem, vmem_limit_bytes=VMEM_LIMIT)


def _resident(shape):
    nd = len(shape)
    return pl.BlockSpec(shape, lambda *_: (0,) * nd, pipeline_mode=pl.Buffered(1))


def _resident_layer(stacked_shape, layer):
    nd = len(stacked_shape) - 1
    return pl.BlockSpec((None,) + tuple(stacked_shape[1:]), lambda *_: (layer,) + (0,) * nd,
                        pipeline_mode=pl.Buffered(1))


def _dot(a, b):
    return jnp.dot(a, b, preferred_element_type=f32)


def _dot_nt(a, b):
    return lax.dot_general(a, b, (((1,), (1,)), ((), ())), preferred_element_type=f32)


def _split2(x):
    hi = x.astype(bf16)
    return hi, (x - hi.astype(f32)).astype(bf16)


def _silu(x):
    return x / (1.0 + jnp.exp(-x))


def _norm_mod(x, g, scale, shift):
    ms = jnp.mean(x * x, axis=-1, keepdims=True)
    y = x * lax.rsqrt(ms + EPS)
    return (y * g) * (1.0 + scale) + shift


def _ada_kernel(c_ref, w_ref, b_ref, o_ref):
    ca = _silu(c_ref[...])
    o_ref[...] = jnp.sum(ca * w_ref[...], axis=0, keepdims=True) + b_ref[...]


def _ada_mod(c, ada_w, ada_b):
    depth, d, n = ada_w.shape
    tn = 1536
    out = pl.pallas_call(
        _ada_kernel,
        out_shape=jax.ShapeDtypeStruct((depth, 1, n), f32),
        grid=(depth, n // tn),
        in_specs=[
            pl.BlockSpec((d, 1), lambda l, j: (0, 0)),
            pl.BlockSpec((None, d, tn), lambda l, j: (l, 0, j)),
            pl.BlockSpec((None, 1, tn), lambda l, j: (l, 0, j)),
        ],
        out_specs=pl.BlockSpec((None, 1, tn), lambda l, j: (l, 0, j)),
        compiler_params=_params("arbitrary", "arbitrary"),
        name="ada_mod",
    )(c.reshape(d, 1), ada_w, ada_b.reshape(depth, 1, n))
    return out.reshape(depth, 6, d)


def _rope_table_kernel(pos_ref, freq_ref, sign_ref, cos_ref, sin_ref, cost_ref, sint_ref):
    ang = pos_ref[...].astype(f32) * freq_ref[...]
    cos = jnp.cos(ang)
    sin = jnp.sin(ang) * sign_ref[...]
    cos_ref[...] = cos
    sin_ref[...] = sin
    cost_ref[...] = cos.T
    sint_ref[...] = sin.T


def _rope_tables(positions, hd):
    s = positions.shape[-1]
    half = hd // 2
    inv_freq = ROPE_THETA ** (-jnp.arange(half, dtype=f32) / half)
    freq = jnp.concatenate([inv_freq, inv_freq]).reshape(1, hd)
    sign = jnp.concatenate([-jnp.ones((half,), f32), jnp.ones((half,), f32)]).reshape(1, hd)
    tm = ROW_TILE
    row = pl.BlockSpec((tm, hd), lambda i: (i, 0))
    col = pl.BlockSpec((hd, tm), lambda i: (0, i))
    return pl.pallas_call(
        _rope_table_kernel,
        out_shape=(jax.ShapeDtypeStruct((s, hd), f32), jax.ShapeDtypeStruct((s, hd), f32),
                   jax.ShapeDtypeStruct((hd, s), f32), jax.ShapeDtypeStruct((hd, s), f32)),
        grid=(s // tm,),
        in_specs=[
            pl.BlockSpec((tm, 1), lambda i: (i, 0)),
            pl.BlockSpec((1, hd), lambda i: (0, 0)),
            pl.BlockSpec((1, hd), lambda i: (0, 0)),
        ],
        out_specs=(row, row, col, col),
        compiler_params=_params("arbitrary"),
        name="rope_tables",
    )(positions.reshape(s, 1), freq, sign)


def _gla_in_kernel(x_ref, mod_ref, ng_ref, w_ref, wglr_ref, wup_ref, bg_ref,
                   q_ref, k_ref, v_ref, r_ref, g_ref, *, hdk, hdv, rank):
    h = _norm_mod(x_ref[...], ng_ref[...], mod_ref[1:2, :], mod_ref[0:1, :]).astype(bf16)
    glr3 = _dot(h, wglr_ref[...])
    hi = glr3.astype(bf16).astype(f32)
    lane = lax.broadcasted_iota(jnp.int32, glr3.shape, 1)
    use_lo = jnp.logical_and(lane >= rank, lane < 2 * rank)
    z = _dot(jnp.where(use_lo, glr3 - hi, hi).astype(bf16), wup_ref[...]) + bg_ref[...]
    log_sig = jnp.minimum(z, 0.0) - jnp.log(1.0 + jnp.exp(-jnp.abs(z)))
    g_ref[...] = log_sig / GLA_GATE_NORM
    q_ref[...] = _dot(h, w_ref[:, 0:hdk].astype(bf16))
    k_ref[...] = _dot(h, w_ref[:, hdk:2 * hdk].astype(bf16))
    v_ref[...] = _dot(h, w_ref[:, 2 * hdk:2 * hdk + hdv].astype(bf16))
    r_ref[...] = _dot(h, w_ref[:, 2 * hdk + hdv:2 * hdk + 2 * hdv].astype(bf16))


def _gla_in(x, mod, ng, w_in_all, layer, w_gate_up, b_gate):
    s, d = x.shape
    w_in = w_in_all[layer]
    hdk = w_gate_up.shape[1]
    rank = w_gate_up.shape[0]
    hdv = (w_in.shape[1] - 2 * hdk - rank) // 2
    n_main = 2 * hdk + 2 * hdv
    assert 3 * rank <= LANES
    w_glr = jnp.pad(jnp.tile(w_in[:, n_main:], (1, 3)), ((0, 0), (0, LANES - 3 * rank))).astype(bf16)
    wup_hi = w_gate_up.astype(bf16)
    wup_lo = (w_gate_up - wup_hi.astype(f32)).astype(bf16)
    wup = jnp.pad(jnp.concatenate([wup_hi, wup_hi, wup_lo], axis=0), ((0, LANES - 3 * rank), (0, 0)))
    tm = ROW_TILE
    row = lambda n: pl.BlockSpec((tm, n), lambda i: (i, 0))
    return pl.pallas_call(
        functools.partial(_gla_in_kernel, hdk=hdk, hdv=hdv, rank=rank),
        out_shape=(jax.ShapeDtypeStruct((s, hdk), f32), jax.ShapeDtypeStruct((s, hdk), f32),
                   jax.ShapeDtypeStruct((s, hdv), f32), jax.ShapeDtypeStruct((s, hdv), f32),
                   jax.ShapeDtypeStruct((s, hdk), f32)),
        grid=(s // tm,),
        in_specs=[row(d), _resident(mod.shape), _resident(ng.shape), _resident_layer(w_in_all.shape, layer),
                  _resident(w_glr.shape), _resident(wup.shape), _resident((1, hdk))],
        out_specs=(row(hdk), row(hdk), row(hdv), row(hdv), row(hdk)),
        compiler_params=_params("arbitrary"),
        name="gla_in_proj",
    )(x, mod, ng, w_in_all, w_glr, wup, b_gate.reshape(1, hdk))


def _gla_core_kernel(q_ref, k_ref, g_ref, v_ref, r_ref, og_ref, o_ref, s_ref,
                     qd_ref, ki_ref, kt_ref, gt_ref, *, heads):
    i = pl.program_id(0)

    @pl.when(i == 0)
    def _():
        s_ref[...] = jnp.zeros_like(s_ref)
        qd_ref[1] = jnp.zeros(qd_ref.shape[1:], f32)
        ki_ref[1] = jnp.zeros(ki_ref.shape[1:], bf16)
        kt_ref[1] = jnp.zeros(kt_ref.shape[1:], f32)
        gt_ref[1] = jnp.zeros(gt_ref.shape[1:], f32)

    t = q_ref.shape[0]
    dk = q_ref.shape[1] // heads
    dv = v_ref.shape[1] // heads
    lc = GLA_CHUNK
    nch = t // lc
    scale = dk ** -0.5

    def step(cur, prev):
        row = lax.broadcasted_iota(jnp.int32, (t, t), 0)
        col = lax.broadcasted_iota(jnp.int32, (t, t), 1)
        tri = jnp.logical_and(col <= row, (row // lc) == (col // lc))
        tri_b = jnp.where(tri, 1.0, 0.0).astype(bf16)
        lane_chunk = lax.broadcasted_iota(jnp.int32, (dk, t), 1) // lc
        row_chunk = lax.broadcasted_iota(jnp.int32, (t, dk), 0) // lc

        g_hi, g_lo = _split2(g_ref[...])
        gc = _dot(tri_b, g_hi) + _dot(tri_b, g_lo)
        g_last = jnp.concatenate(
            [jnp.broadcast_to(gc[c * lc + lc - 1:c * lc + lc, :], (lc, heads * dk)) for c in range(nch)], axis=0)
        k_all = k_ref[...]
        qd_ref[cur] = q_ref[...] * scale * jnp.exp(gc)
        ki_ref[cur] = (k_all * jnp.exp(-gc)).astype(bf16)
        kt_ref[cur] = (k_all * jnp.exp(g_last - gc)).T
        gt_ref[cur] = gc.T

        for h in range(heads):
            ks = slice(h * dk, (h + 1) * dk)
            vs = slice(h * dv, (h + 1) * dv)
            q_dec = qd_ref[prev, :, ks]
            k_tail_t = kt_ref[prev, ks, :]
            vh = v_ref[:, vs].astype(bf16)
            a = jnp.where(tri, _dot_nt(q_dec.astype(bf16), ki_ref[prev, :, ks]), 0.0).astype(bf16)
            kt_stack = jnp.concatenate(
                [jnp.where(lane_chunk == c, k_tail_t, 0.0).astype(bf16) for c in range(nch)], axis=0)
            u_all = _dot(kt_stack, vh)
            state = s_ref[h]
            states = []
            for c in range(nch):
                states.append(state.astype(bf16))
                decay = jnp.exp(gt_ref[prev, ks, c * lc + lc - 1:c * lc + lc])
                state = decay * state + u_all[c * dk:(c + 1) * dk, :]
            s_ref[h] = state
            q_blocks = [jnp.where(row_chunk == c, q_dec, 0.0).astype(bf16) for c in range(nch)]
            o = _dot(jnp.concatenate([a] + q_blocks, axis=1), jnp.concatenate([vh] + states, axis=0))
            ms = jnp.mean(o * o, axis=-1, keepdims=True)
            on = o * lax.rsqrt(ms + EPS) * og_ref[...]
            o_ref[:, vs] = (on * _silu(r_ref[:, vs])).astype(bf16)

    @pl.when(i % 2 == 0)
    def _():
        step(0, 1)

    @pl.when(i % 2 == 1)
    def _():
        step(1, 0)


def _gla_core(q, k, g, v, r, onorm_g):
    s, hdk = q.shape
    hdv = v.shape[1]
    heads = GLA_HEADS
    t = GLA_STEP
    n = s // t
    cur = lambda w: pl.BlockSpec((t, w), lambda i: (jnp.minimum(i, n - 1), 0))
    prev = lambda w: pl.BlockSpec((t, w), lambda i: (jnp.maximum(i - 1, 0), 0))
    return pl.pallas_call(
        functools.partial(_gla_core_kernel, heads=heads),
        out_shape=jax.ShapeDtypeStruct((s, hdv), bf16),
        grid=(n + 1,),
        in_specs=[cur(hdk), cur(hdk), cur(hdk), prev(hdv), prev(hdv), _resident((1, hdv // heads))],
        out_specs=prev(hdv),
        scratch_shapes=[pltpu.VMEM((heads, hdk // heads, hdv // heads), f32),
                        pltpu.VMEM((2, t, hdk), f32), pltpu.VMEM((2, t, hdk), bf16),
                        pltpu.VMEM((2, hdk, t), f32), pltpu.VMEM((2, hdk, t), f32)],
        compiler_params=_params("arbitrary"),
        name="gla_core",
    )(q, k, g, v, r, onorm_g.reshape(1, hdv // heads))


def _moba_in_kernel(x_ref, mod_ref, ng_ref, wqt_ref, wk_ref, wvt_ref, qg_ref, kg_ref, cos_ref, sin_ref,
                    cost_ref, sint_ref, qt_ref, k_ref, vt_ref, bias_ref, yq_ref, yk_ref, kmean_ref, *, heads):
    i = pl.program_id(0)
    tm = x_ref.shape[0]
    d_attn = qt_ref.shape[0]
    hd = d_attn // heads
    half = hd // 2

    @pl.when(i == 0)
    def _():
        yq_ref[1] = jnp.zeros(yq_ref.shape[1:], f32)
        yk_ref[1] = jnp.zeros(yk_ref.shape[1:], f32)
        kmean_ref[...] = jnp.zeros_like(kmean_ref)

    def step(cur, prev):
        hf = _norm_mod(x_ref[...], ng_ref[...], mod_ref[1:2, :], mod_ref[0:1, :])
        h = hf.astype(bf16)
        ht = hf.T.astype(bf16)
        yq_ref[cur] = _dot(wqt_ref[...], ht)
        yk_ref[cur] = _dot(h, wk_ref[...])
        vt_ref[...] = _dot(wvt_ref[...], ht).astype(bf16)

        cos_t = cost_ref[0:half, :]
        sin_t = sint_ref[half:hd, :]
        for hh in range(heads):
            y = yq_ref[prev, hh * hd:(hh + 1) * hd, :]
            yn = y * lax.rsqrt(jnp.mean(y * y, axis=0, keepdims=True) + EPS) * qg_ref[...]
            t1, t2 = yn[0:half, :], yn[half:hd, :]
            qt_ref[hh * hd:hh * hd + half, :] = t1 * cos_t - t2 * sin_t
            qt_ref[hh * hd + half:(hh + 1) * hd, :] = t2 * cos_t + t1 * sin_t

        blk0 = jnp.maximum(i - 1, 0) * (tm // MOBA_BLOCK)
        row_blk = blk0 + lax.broadcasted_iota(jnp.int32, (tm, hd), 0) // MOBA_BLOCK
        col = lax.broadcasted_iota(jnp.int32, (tm, hd), 1)
        extra = jnp.where(col == row_blk, 1.0, 0.0).astype(bf16)
        cos = cos_ref[...]
        sin = sin_ref[...]
        for hh in range(heads):
            hs = slice(hh * hd, (hh + 1) * hd)
            y = yk_ref[prev, :, hs]
            yn = y * lax.rsqrt(jnp.mean(y * y, axis=-1, keepdims=True) + EPS) * kg_ref[...]
            kr = yn * cos + pltpu.roll(yn, half, 1) * sin
            k_ref[:, 2 * hh * hd:(2 * hh + 1) * hd] = kr.astype(bf16)
            k_ref[:, (2 * hh + 1) * hd:(2 * hh + 2) * hd] = extra
            grp = pl.multiple_of((blk0 // SUBLANES) * SUBLANES, SUBLANES)
            rows = kmean_ref[pl.ds(grp, SUBLANES), hs]
            rid = grp + lax.broadcasted_iota(jnp.int32, rows.shape, 0)
            for b in range(tm // MOBA_BLOCK):
                mean_b = jnp.mean(kr[b * MOBA_BLOCK:(b + 1) * MOBA_BLOCK, :], axis=0, keepdims=True)
                rows = jnp.where(rid == blk0 + b, mean_b, rows)
            kmean_ref[pl.ds(grp, SUBLANES), hs] = rows

        nb = kmean_ref.shape[0]
        own_blk = blk0 + lax.broadcasted_iota(jnp.int32, (1, tm), 1) // MOBA_BLOCK
        blk = lax.broadcasted_iota(jnp.int32, (nb, tm), 0)
        for hh in range(heads):
            hs = slice(hh * hd, (hh + 1) * hd)
            bias = _moba_gate_bias(kmean_ref.at[:, hs], qt_ref[hs, :], own_blk, MOBA_MASK_VALUE)
            bias_ref[hh] = jnp.where(blk == own_blk, 0.0, bias).astype(bf16)

    @pl.when(i % 2 == 0)
    def _():
        step(0, 1)

    @pl.when(i % 2 == 1)
    def _():
        step(1, 0)


def _moba_in(x, mod, ng, w_in, qg, kg, tables):
    s, d = x.shape
    heads = MOBA_HEADS
    d_attn = w_in.shape[1] // 3
    hd = d_attn // heads
    tm = ROW_TILE
    nb = s // MOBA_BLOCK
    n = s // tm
    assert nb <= hd and nb % SUBLANES == 0 and SUBLANES % (tm // MOBA_BLOCK) == 0
    cos_t, sin_t, cos_tt, sin_tt = tables
    cur = lambda i: jnp.minimum(i, n - 1)
    prev = lambda i: jnp.maximum(i - 1, 0)
    wqt = w_in[:, 0:d_attn].T.astype(bf16)
    wk = w_in[:, d_attn:2 * d_attn].astype(bf16)
    wvt = w_in[:, 2 * d_attn:].T.astype(bf16)
    return pl.pallas_call(
        functools.partial(_moba_in_kernel, heads=heads),
        out_shape=(jax.ShapeDtypeStruct((d_attn, s), f32), jax.ShapeDtypeStruct((s, 2 * d_attn), bf16),
                   jax.ShapeDtypeStruct((d_attn, s), bf16), jax.ShapeDtypeStruct((heads, nb, s), bf16)),
        grid=(n + 1,),
        in_specs=[pl.BlockSpec((tm, d), lambda i: (cur(i), 0)), _resident(mod.shape), _resident(ng.shape),
                  _resident(wqt.shape), _resident(wk.shape), _resident(wvt.shape), _resident((hd, 1)),
                  _resident((1, hd)),
                  pl.BlockSpec((tm, hd), lambda i: (prev(i), 0)), pl.BlockSpec((tm, hd), lambda i: (prev(i), 0)),
                  pl.BlockSpec((hd, tm), lambda i: (0, prev(i))), pl.BlockSpec((hd, tm), lambda i: (0, prev(i)))],
        out_specs=(pl.BlockSpec((d_attn, tm), lambda i: (0, prev(i))),
                   pl.BlockSpec((tm, 2 * d_attn), lambda i: (prev(i), 0)),
                   pl.BlockSpec((d_attn, tm), lambda i: (0, cur(i))),
                   pl.BlockSpec((heads, nb, tm), lambda i: (0, 0, prev(i)))),
        scratch_shapes=[pltpu.VMEM((2, d_attn, tm), f32), pltpu.VMEM((2, tm, d_attn), f32),
                        pltpu.VMEM((nb, d_attn), f32)],
        compiler_params=_params("arbitrary"),
        name="moba_in_proj",
    )(x, mod, ng, wqt, wk, wvt, qg.reshape(hd, 1), kg.reshape(1, hd), cos_t, sin_t, cos_tt, sin_tt)


def _moba_gate_bias(kmean_ref, qt, own_blk, off_value):
    nb = kmean_ref.shape[0]
    tq = qt.shape[1]
    neg = jnp.float32(-jnp.inf)
    km = kmean_ref[...]
    km_hi = km.astype(bf16)
    km_lo = (km - km_hi.astype(f32)).astype(bf16)
    qt_hi = qt.astype(bf16)
    qt_lo = (qt - qt_hi.astype(f32)).astype(bf16)
    gate = _dot(km_hi, qt_hi) + _dot(km_lo, qt_hi) + _dot(km_hi, qt_lo)
    blk = lax.broadcasted_iota(jnp.int32, (nb, tq), 0)
    gate = jnp.where(blk < own_blk, gate, neg)
    bias = jnp.full((nb, tq), off_value, f32)
    for _ in range(MOBA_TOPK):
        top = jnp.max(gate, axis=0, keepdims=True)
        first = jnp.min(jnp.where(gate == top, blk, nb), axis=0, keepdims=True)
        first = jnp.where(top > neg, first, nb)
        pick = blk == first
        bias = jnp.where(pick, 0.0, bias)
        gate = jnp.where(pick, neg, gate)
    return bias


def _causal_own_scores(k_ref, qb, blk):
    bs = MOBA_BLOCK
    hd = qb.shape[0]
    st = _dot(k_ref[pl.ds(pl.multiple_of(blk * bs, bs), bs), 0:hd], qb)
    key_pos = lax.broadcasted_iota(jnp.int32, (bs, bs), 0)
    qry_pos = lax.broadcasted_iota(jnp.int32, (bs, bs), 1)
    return jnp.where(key_pos <= qry_pos, st, jnp.float32(-jnp.inf))


def _moba_attn_fast_kernel(qt_ref, k_ref, vt_ref, bias_ref, o_ref, acc_ref, l_ref, qaug_ref):
    hd, tq = qt_ref.shape
    bs = MOBA_BLOCK
    nt = tq // bs
    nb = bias_ref.shape[0]
    first = pl.program_id(1) * nt
    qaug_ref[0:hd, :] = (qt_ref[...] * (hd ** -0.5 * LOG2E)).astype(bf16)
    qaug_ref[hd:hd + nb, :] = bias_ref[...]
    qaug_ref[hd + nb:2 * hd, :] = jnp.zeros((hd - nb, tq), bf16)
    acc_ref[...] = jnp.zeros_like(acc_ref)
    l_ref[...] = jnp.zeros_like(l_ref)

    def group(g, nblk, causal):
        gk = nblk * bs
        off = pl.multiple_of(g * gk, gk)
        sg = _dot(k_ref[pl.ds(off, gk), :], qaug_ref[...])
        if causal:
            key_pos = off + lax.broadcasted_iota(jnp.int32, (gk, tq), 0)
            qry_pos = first * bs + lax.broadcasted_iota(jnp.int32, (gk, tq), 1)
            sg = jnp.where(key_pos <= qry_pos, sg, jnp.float32(-jnp.inf))
        pg = jnp.exp2(sg)
        l_ref[...] += jnp.sum(pg, axis=0, keepdims=True)
        acc_ref[...] += _dot(vt_ref[:, pl.ds(off, gk)], pg.astype(bf16))

    def past_group(g, carry):
        group(g, MOBA_GROUP, False)
        return carry

    def own_group(g, carry):
        group(g, nt, True)
        return carry

    n_past = first // MOBA_GROUP

    def past_pair(u, carry):
        group(2 * u, MOBA_GROUP, False)
        group(2 * u + 1, MOBA_GROUP, False)
        return carry

    lax.fori_loop(0, n_past // 2, past_pair, 0)
    lax.fori_loop(2 * (n_past // 2), n_past, past_group, 0)
    fills_group = (first + nt) % MOBA_GROUP == 0

    @pl.when(fills_group)
    def _():
        group(n_past, MOBA_GROUP, True)

    @pl.when(jnp.logical_not(fills_group))
    def _():
        lax.fori_loop(n_past * (MOBA_GROUP // nt), first // nt + 1, own_group, 0)

    o_ref[...] = (acc_ref[...] / l_ref[...]).T.astype(bf16)


def _moba_attn_general_kernel(qt_ref, k_ref, vt_ref, sel_ref, o_ref, acc_ref, l_ref, m_ref, bias_ref):
    hd, tq = qt_ref.shape
    bs = MOBA_BLOCK
    nt = tq // bs
    for a in range(nt):
        ls = slice(a * bs, (a + 1) * bs)
        i = pl.program_id(1) * nt + a
        bias_ref[...] = jnp.where(sel_ref[:, ls].astype(f32) == 0.0, 0.0, -jnp.inf)
        qb = (qt_ref[:, ls] * (hd ** -0.5 * LOG2E)).astype(bf16)
        st = _causal_own_scores(k_ref, qb, i)
        m0 = jnp.max(st, axis=0, keepdims=True)
        p = jnp.exp2(st - m0)
        m_ref[...] = m0
        l_ref[:, ls] = jnp.sum(p, axis=0, keepdims=True)
        own = pl.multiple_of(i * bs, bs)
        acc_ref[:, ls] = _dot(vt_ref[:, pl.ds(own, bs)], p.astype(bf16))

        def past_block(j, carry, ls=ls, qb=qb):
            off = pl.multiple_of(j * bs, bs)
            sj = _dot(k_ref[pl.ds(off, bs), 0:hd], qb) + bias_ref[pl.ds(j, 1), :]
            m_old = m_ref[...]
            m_new = jnp.maximum(m_old, jnp.max(sj, axis=0, keepdims=True))
            alpha = jnp.exp2(m_old - m_new)
            pj = jnp.exp2(sj - m_new)
            l_ref[:, ls] = alpha * l_ref[:, ls] + jnp.sum(pj, axis=0, keepdims=True)
            acc_ref[:, ls] = alpha * acc_ref[:, ls] + _dot(vt_ref[:, pl.ds(off, bs)], pj.astype(bf16))
            m_ref[...] = m_new
            return carry

        lax.fori_loop(0, i, past_block, 0)
        o_ref[ls, :] = (acc_ref[:, ls] / l_ref[:, ls]).T.astype(bf16)


def _moba_attn(qt, k, vt, bias, qg, kg):
    d_attn, s = qt.shape
    heads = MOBA_HEADS
    hd = d_attn // heads
    tq = MOBA_QTILE
    bs = MOBA_BLOCK
    nb = s // bs
    assert nb % MOBA_GROUP == 0 and s % tq == 0 and tq % bs == 0 and MOBA_GROUP % (tq // bs) == 0

    def call(body, scratch):
        return pl.pallas_call(
            body,
            out_shape=jax.ShapeDtypeStruct((s, d_attn), bf16),
            grid=(heads, s // tq),
            in_specs=[
                pl.BlockSpec((hd, tq), lambda h, t: (h, t)),
                pl.BlockSpec((s, 2 * hd), lambda h, t: (0, h)),
                pl.BlockSpec((hd, s), lambda h, t: (h, 0)),
                pl.BlockSpec((None, nb, tq), lambda h, t: (h, 0, t)),
            ],
            out_specs=pl.BlockSpec((tq, hd), lambda h, t: (t, h)),
            scratch_shapes=scratch,
            compiler_params=_params("arbitrary", "arbitrary"),
            name=body.__name__.strip("_"),
        )(qt, k, vt, bias)

    stats = [pltpu.VMEM((hd, tq), f32), pltpu.VMEM((1, tq), f32)]
    fast = lambda: call(_moba_attn_fast_kernel, stats + [pltpu.VMEM((2 * hd, tq), bf16)])
    general = lambda: call(_moba_attn_general_kernel,
                           stats + [pltpu.VMEM((1, bs), f32), pltpu.VMEM((nb, bs), f32)])
    bound = 1.02 * hd ** 0.5 * LOG2E * jnp.max(jnp.abs(qg)) * jnp.max(jnp.abs(kg))
    return lax.cond(bound <= MOBA_FAST_RANGE, fast, general)


def _out_mlp_kernel(a_ref, x_ref, mod_ref, ng_ref, wo_ref, w1_ref, w2_ref, o_ref, acc_ref):
    x1 = x_ref[...] + mod_ref[2:3, :] * _dot(a_ref[...], wo_ref[...].astype(bf16))
    h = _norm_mod(x1, ng_ref[...], mod_ref[4:5, :], mod_ref[3:4, :]).astype(bf16)
    d_ff = w1_ref.shape[1]
    for c in range(d_ff // FF_CHUNK):
        cs = slice(c * FF_CHUNK, (c + 1) * FF_CHUNK)
        u = jnp.maximum(_dot(h, w1_ref[:, cs].astype(bf16)), 0.0)
        part = _dot((u * u).astype(bf16), w2_ref[cs, :].astype(bf16))
        if c == 0:
            acc_ref[...] = part
        else:
            acc_ref[...] += part
    o_ref[...] = x1 + mod_ref[5:6, :] * acc_ref[...]


def _out_mlp(a, x, mod, ng, w_out_all, mixer_layer, w1_all, w2_all, layer):
    s, d = x.shape
    tm = ROW_TILE
    row = lambda n: pl.BlockSpec((tm, n), lambda i: (i, 0))
    return pl.pallas_call(
        _out_mlp_kernel,
        out_shape=jax.ShapeDtypeStruct((s, d), f32),
        grid=(s // tm,),
        in_specs=[row(a.shape[1]), row(d), _resident(mod.shape), _resident(ng.shape),
                  _resident_layer(w_out_all.shape, mixer_layer), _resident_layer(w1_all.shape, layer),
                  _resident_layer(w2_all.shape, layer)],
        out_specs=row(d),
        scratch_shapes=[pltpu.VMEM((tm, d), f32)],
        compiler_params=_params("arbitrary"),
        name="out_proj_mlp",
    )(a, x, mod, ng, w_out_all, w1_all, w2_all)


def kernel(x, c, positions, ada_w, ada_b, norm_mix_g, norm_mlp_g, gla_w_in, gla_w_gate_up, gla_b_gate,
           gla_onorm_g, gla_w_out, moba_w_in, moba_q_norm_g, moba_k_norm_g, moba_w_out, mlp_w1, mlp_w2):
    b, s, d = x.shape
    assert b == 1 and s % ROW_TILE == 0 and s % MOBA_BLOCK == 0
    depth = ada_w.shape[0]
    xs = x.reshape(s, d)
    mod = _ada_mod(c, ada_w, ada_b)
    moba_hd = moba_w_out.shape[1] // MOBA_HEADS
    tables = _rope_tables(positions, moba_hd)
    for i in range(depth):
        j = i // 2
        ng = norm_mix_g[i].reshape(1, d)
        if i % 2 == 0:
            q, k, v, r, g = _gla_in(xs, mod[i], ng, gla_w_in, j, gla_w_gate_up[j], gla_b_gate[j])
            a = _gla_core(q, k, g, v, r, gla_onorm_g[j])
            w_out = gla_w_out
        else:
            q, k, vt, bias = _moba_in(xs, mod[i], ng, moba_w_in[j], moba_q_norm_g[j], moba_k_norm_g[j], tables)
            a = _moba_attn(q, k, vt, bias, moba_q_norm_g[j], moba_k_norm_g[j])
            w_out = moba_w_out
        xs = _out_mlp(a, xs, mod[i], norm_mlp_g[i].reshape(1, d), w_out, j, mlp_w1, mlp_w2, i)
    return xs.reshape(b, s, d)
```

```python
import functools

import jax
import jax.numpy as jnp
from jax import lax
from jax.experimental import pallas as pl
from jax.experimental.pallas import tpu as pltpu

f32 = jnp.float32
bf16 = jnp.bfloat16

EPS = 1e-6
GLA_HEADS = 4
GLA_GATE_NORM = 16.0
GLA_CHUNK = 64
MOBA_HEADS = 8
MOBA_BLOCK = 256
MOBA_TOPK = 3
MOBA_GROUP = 8
MOBA_QTILE = 1024
LOG2E = 1.4426950408889634
MOBA_MASK_VALUE = -2.0 ** 30
MOBA_FAST_RANGE = 32.0
ROPE_THETA = 10000.0

LANES = 128
SUBLANES = 8
VMEM_LIMIT = 56 * 1024 * 1024

ROW_TILE = 512
GLA_STEP = 256
FF_CHUNK = 1024


def _params(*sem):
    return pltpu.CompilerParams(dimension_semantics=sem, vmem_limit_bytes=VMEM_LIMIT)


def _resident(shape):
    nd = len(shape)
    return pl.BlockSpec(shape, lambda *_: (0,) * nd, pipeline_mode=pl.Buffered(1))


def _resident_layer(stacked_shape, layer):
    nd = len(stacked_shape) - 1
    return pl.BlockSpec((None,) + tuple(stacked_shape[1:]), lambda *_: (layer,) + (0,) * nd,
                        pipeline_mode=pl.Buffered(1))


def _dot(a, b):
    return jnp.dot(a, b, preferred_element_type=f32)


def _dot_nt(a, b):
    return lax.dot_general(a, b, (((1,), (1,)), ((), ())), preferred_element_type=f32)


def _split2(x):
    hi = x.astype(bf16)
    return hi, (x - hi.astype(f32)).astype(bf16)


def _silu(x):
    return x / (1.0 + jnp.exp(-x))


def _norm_mod(x, g, scale, shift):
    ms = jnp.mean(x * x, axis=-1, keepdims=True)
    y = x * lax.rsqrt(ms + EPS)
    return (y * g) * (1.0 + scale) + shift


def _ada_kernel(c_ref, w_ref, b_ref, o_ref):
    ca = _silu(c_ref[...])
    o_ref[...] = jnp.sum(ca * w_ref[...], axis=0, keepdims=True) + b_ref[...]


def _ada_mod(c, ada_w, ada_b):
    depth, d, n = ada_w.shape
    tn = 1536
    out = pl.pallas_call(
        _ada_kernel,
        out_shape=jax.ShapeDtypeStruct((depth, 1, n), f32),
        grid=(depth, n // tn),
        in_specs=[
            pl.BlockSpec((d, 1), lambda l, j: (0, 0)),
            pl.BlockSpec((None, d, tn), lambda l, j: (l, 0, j)),
            pl.BlockSpec((None, 1, tn), lambda l, j: (l, 0, j)),
        ],
        out_specs=pl.BlockSpec((None, 1, tn), lambda l, j: (l, 0, j)),
        compiler_params=_params("arbitrary", "arbitrary"),
        name="ada_mod",
    )(c.reshape(d, 1), ada_w, ada_b.reshape(depth, 1, n))
    return out.reshape(depth, 6, d)


def _rope_table_kernel(pos_ref, freq_ref, sign_ref, cos_ref, sin_ref, cost_ref, sint_ref):
    ang = pos_ref[...].astype(f32) * freq_ref[...]
    cos = jnp.cos(ang)
    sin = jnp.sin(ang) * sign_ref[...]
    cos_ref[...] = cos
    sin_ref[...] = sin
    cost_ref[...] = cos.T
    sint_ref[...] = sin.T


def _rope_tables(positions, hd):
    s = positions.shape[-1]
    half = hd // 2
    inv_freq = ROPE_THETA ** (-jnp.arange(half, dtype=f32) / half)
    freq = jnp.concatenate([inv_freq, inv_freq]).reshape(1, hd)
    sign = jnp.concatenate([-jnp.ones((half,), f32), jnp.ones((half,), f32)]).reshape(1, hd)
    tm = ROW_TILE
    row = pl.BlockSpec((tm, hd), lambda i: (i, 0))
    col = pl.BlockSpec((hd, tm), lambda i: (0, i))
    return pl.pallas_call(
        _rope_table_kernel,
        out_shape=(jax.ShapeDtypeStruct((s, hd), f32), jax.ShapeDtypeStruct((s, hd), f32),
                   jax.ShapeDtypeStruct((hd, s), f32), jax.ShapeDtypeStruct((hd, s), f32)),
        grid=(s // tm,),
        in_specs=[
            pl.BlockSpec((tm, 1), lambda i: (i, 0)),
            pl.BlockSpec((1, hd), lambda i: (0, 0)),
            pl.BlockSpec((1, hd), lambda i: (0, 0)),
        ],
        out_specs=(row, row, col, col),
        compiler_params=_params("arbitrary"),
        name="rope_tables",
    )(positions.reshape(s, 1), freq, sign)


def _gla_in_kernel(x_ref, mod_ref, ng_ref, w_ref, wglr_ref, wup_ref, bg_ref,
                   q_ref, k_ref, v_ref, r_ref, g_ref, *, hdk, hdv, rank):
    h = _norm_mod(x_ref[...], ng_ref[...], mod_ref[1:2, :], mod_ref[0:1, :]).astype(bf16)
    glr3 = _dot(h, wglr_ref[...])
    hi = glr3.astype(bf16).astype(f32)
    lane = lax.broadcasted_iota(jnp.int32, glr3.shape, 1)
    use_lo = jnp.logical_and(lane >= rank, lane < 2 * rank)
    z = _dot(jnp.where(use_lo, glr3 - hi, hi).astype(bf16), wup_ref[...]) + bg_ref[...]
    log_sig = jnp.minimum(z, 0.0) - jnp.log(1.0 + jnp.exp(-jnp.abs(z)))
    g_ref[...] = log_sig / GLA_GATE_NORM
    q_ref[...] = _dot(h, w_ref[:, 0:hdk].astype(bf16))
    k_ref[...] = _dot(h, w_ref[:, hdk:2 * hdk].astype(bf16))
    v_ref[...] = _dot(h, w_ref[:, 2 * hdk:2 * hdk + hdv].astype(bf16))
    r_ref[...] = _dot(h, w_ref[:, 2 * hdk + hdv:2 * hdk + 2 * hdv].astype(bf16))


def _gla_in(x, mod, ng, w_in_all, layer, w_gate_up, b_gate):
    s, d = x.shape
    w_in = w_in_all[layer]
    hdk = w_gate_up.shape[1]
    rank = w_gate_up.shape[0]
    hdv = (w_in.shape[1] - 2 * hdk - rank) // 2
    n_main = 2 * hdk + 2 * hdv
    assert 3 * rank <= LANES
    w_glr = jnp.pad(jnp.tile(w_in[:, n_main:], (1, 3)), ((0, 0), (0, LANES - 3 * rank))).astype(bf16)
    wup_hi = w_gate_up.astype(bf16)
    wup_lo = (w_gate_up - wup_hi.astype(f32)).astype(bf16)
    wup = jnp.pad(jnp.concatenate([wup_hi, wup_hi, wup_lo], axis=0), ((0, LANES - 3 * rank), (0, 0)))
    tm = ROW_TILE
    row = lambda n: pl.BlockSpec((tm, n), lambda i: (i, 0))
    return pl.pallas_call(
        functools.partial(_gla_in_kernel, hdk=hdk, hdv=hdv, rank=rank),
        out_shape=(jax.ShapeDtypeStruct((s, hdk), f32), jax.ShapeDtypeStruct((s, hdk), f32),
                   jax.ShapeDtypeStruct((s, hdv), f32), jax.ShapeDtypeStruct((s, hdv), f32),
                   jax.ShapeDtypeStruct((s, hdk), f32)),
        grid=(s // tm,),
        in_specs=[row(d), _resident(mod.shape), _resident(ng.shape), _resident_layer(w_in_all.shape, layer),
                  _resident(w_glr.shape), _resident(wup.shape), _resident((1, hdk))],
        out_specs=(row(hdk), row(hdk), row(hdv), row(hdv), row(hdk)),
        compiler_params=_params("arbitrary"),
        name="gla_in_proj",
    )(x, mod, ng, w_in_all, w_glr, wup, b_gate.reshape(1, hdk))


def _gla_core_kernel(q_ref, k_ref, g_ref, v_ref, r_ref, og_ref, o_ref, s_ref,
                     qd_ref, ki_ref, kt_ref, gt_ref, *, heads):
    i = pl.program_id(0)

    @pl.when(i == 0)
    def _():
        s_ref[...] = jnp.zeros_like(s_ref)
        qd_ref[1] = jnp.zeros(qd_ref.shape[1:], f32)
        ki_ref[1] = jnp.zeros(ki_ref.shape[1:], bf16)
        kt_ref[1] = jnp.zeros(kt_ref.shape[1:], f32)
        gt_ref[1] = jnp.zeros(gt_ref.shape[1:], f32)

    t = q_ref.shape[0]
    dk = q_ref.shape[1] // heads
    dv = v_ref.shape[1] // heads
    lc = GLA_CHUNK
    nch = t // lc
    scale = dk ** -0.5

    def step(cur, prev):
        row = lax.broadcasted_iota(jnp.int32, (t, t), 0)
        col = lax.broadcasted_iota(jnp.int32, (t, t), 1)
        tri = jnp.logical_and(col <= row, (row // lc) == (col // lc))
        tri_b = jnp.where(tri, 1.0, 0.0).astype(bf16)
        lane_chunk = lax.broadcasted_iota(jnp.int32, (dk, t), 1) // lc
        row_chunk = lax.broadcasted_iota(jnp.int32, (t, dk), 0) // lc

        g_hi, g_lo = _split2(g_ref[...])
        gc = _dot(tri_b, g_hi) + _dot(tri_b, g_lo)
        g_last = jnp.concatenate(
            [jnp.broadcast_to(gc[c * lc + lc - 1:c * lc + lc, :], (lc, heads * dk)) for c in range(nch)], axis=0)
        k_all = k_ref[...]
        qd_ref[cur] = q_ref[...] * scale * jnp.exp(gc)
        ki_ref[cur] = (k_all * jnp.exp(-gc)).astype(bf16)
        kt_ref[cur] = (k_all * jnp.exp(g_last - gc)).T
        gt_ref[cur] = gc.T

        for h in range(heads):
            ks = slice(h * dk, (h + 1) * dk)
            vs = slice(h * dv, (h + 1) * dv)
            q_dec = qd_ref[prev, :, ks]
            k_tail_t = kt_ref[prev, ks, :]
            vh = v_ref[:, vs].astype(bf16)
            a = jnp.where(tri, _dot_nt(q_dec.astype(bf16), ki_ref[prev, :, ks]), 0.0).astype(bf16)
            kt_stack = jnp.concatenate(
                [jnp.where(lane_chunk == c, k_tail_t, 0.0).astype(bf16) for c in range(nch)], axis=0)
            u_all = _dot(kt_stack, vh)
            state = s_ref[h]
            states = []
            for c in range(nch):
                states.append(state.astype(bf16))
                decay = jnp.exp(gt_ref[prev, ks, c * lc + lc - 1:c * lc + lc])
                state = decay * state + u_all[c * dk:(c + 1) * dk, :]
            s_ref[h] = state
            q_blocks = [jnp.where(row_chunk == c, q_dec, 0.0).astype(bf16) for c in range(nch)]
            o = _dot(jnp.concatenate([a] + q_blocks, axis=1), jnp.concatenate([vh] + states, axis=0))
            ms = jnp.mean(o * o, axis=-1, keepdims=True)
            on = o * lax.rsqrt(ms + EPS) * og_ref[...]
            o_ref[:, vs] = (on * _silu(r_ref[:, vs])).astype(bf16)

    @pl.when(i % 2 == 0)
    def _():
        step(0, 1)

    @pl.when(i % 2 == 1)
    def _():
        step(1, 0)


def _gla_core(q, k, g, v, r, onorm_g):
    s, hdk = q.shape
    hdv = v.shape[1]
    heads = GLA_HEADS
    t = GLA_STEP
    n = s // t
    cur = lambda w: pl.BlockSpec((t, w), lambda i: (jnp.minimum(i, n - 1), 0))
    prev = lambda w: pl.BlockSpec((t, w), lambda i: (jnp.maximum(i - 1, 0), 0))
    return pl.pallas_call(
        functools.partial(_gla_core_kernel, heads=heads),
        out_shape=jax.ShapeDtypeStruct((s, hdv), bf16),
        grid=(n + 1,),
        in_specs=[cur(hdk), cur(hdk), cur(hdk), prev(hdv), prev(hdv), _resident((1, hdv // heads))],
        out_specs=prev(hdv),
        scratch_shapes=[pltpu.VMEM((heads, hdk // heads, hdv // heads), f32),
                        pltpu.VMEM((2, t, hdk), f32), pltpu.VMEM((2, t, hdk), bf16),
                        pltpu.VMEM((2, hdk, t), f32), pltpu.VMEM((2, hdk, t), f32)],
        compiler_params=_params("arbitrary"),
        name="gla_core",
    )(q, k, g, v, r, onorm_g.reshape(1, hdv // heads))


def _moba_in_kernel(x_ref, mod_ref, ng_ref, wqt_ref, wk_ref, wvt_ref, qg_ref, kg_ref, cos_ref, sin_ref,
                    cost_ref, sint_ref, qt_ref, k_ref, vt_ref, bias_ref, yq_ref, yk_ref, kmean_ref, *, heads):
    i = pl.program_id(0)
    tm = x_ref.shape[0]
    d_attn = qt_ref.shape[0]
    hd = d_attn // heads
    half = hd // 2

    @pl.when(i == 0)
    def _():
        yq_ref[1] = jnp.zeros(yq_ref.shape[1:], f32)
        yk_ref[1] = jnp.zeros(yk_ref.shape[1:], f32)
        kmean_ref[...] = jnp.zeros_like(kmean_ref)

    def step(cur, prev):
        hf = _norm_mod(x_ref[...], ng_ref[...], mod_ref[1:2, :], mod_ref[0:1, :])
        h = hf.astype(bf16)
        ht = hf.T.astype(bf16)
        yq_ref[cur] = _dot(wqt_ref[...], ht)
        yk_ref[cur] = _dot(h, wk_ref[...])
        vt_ref[...] = _dot(wvt_ref[...], ht).astype(bf16)

        cos_t = cost_ref[0:half, :]
        sin_t = sint_ref[half:hd, :]
        for hh in range(heads):
            y = yq_ref[prev, hh * hd:(hh + 1) * hd, :]
            yn = y * lax.rsqrt(jnp.mean(y * y, axis=0, keepdims=True) + EPS) * qg_ref[...]
            t1, t2 = yn[0:half, :], yn[half:hd, :]
            qt_ref[hh * hd:hh * hd + half, :] = t1 * cos_t - t2 * sin_t
            qt_ref[hh * hd + half:(hh + 1) * hd, :] = t2 * cos_t + t1 * sin_t

        blk0 = jnp.maximum(i - 1, 0) * (tm // MOBA_BLOCK)
        row_blk = blk0 + lax.broadcasted_iota(jnp.int32, (tm, hd), 0) // MOBA_BLOCK
        col = lax.broadcasted_iota(jnp.int32, (tm, hd), 1)
        extra = jnp.where(col == row_blk, 1.0, 0.0).astype(bf16)
        cos = cos_ref[...]
        sin = sin_ref[...]
        for hh in range(heads):
            hs = slice(hh * hd, (hh + 1) * hd)
            y = yk_ref[prev, :, hs]
            yn = y * lax.rsqrt(jnp.mean(y * y, axis=-1, keepdims=True) + EPS) * kg_ref[...]
            kr = yn * cos + pltpu.roll(yn, half, 1) * sin
            k_ref[:, 2 * hh * hd:(2 * hh + 1) * hd] = kr.astype(bf16)
            k_ref[:, (2 * hh + 1) * hd:(2 * hh + 2) * hd] = extra
            grp = pl.multiple_of((blk0 // SUBLANES) * SUBLANES, SUBLANES)
            rows = kmean_ref[pl.ds(grp, SUBLANES), hs]
            rid = grp + lax.broadcasted_iota(jnp.int32, rows.shape, 0)
            for b in range(tm // MOBA_BLOCK):
                mean_b = jnp.mean(kr[b * MOBA_BLOCK:(b + 1) * MOBA_BLOCK, :], axis=0, keepdims=True)
                rows = jnp.where(rid == blk0 + b, mean_b, rows)
            kmean_ref[pl.ds(grp, SUBLANES), hs] = rows

        nb = kmean_ref.shape[0]
        own_blk = blk0 + lax.broadcasted_iota(jnp.int32, (1, tm), 1) // MOBA_BLOCK
        blk = lax.broadcasted_iota(jnp.int32, (nb, tm), 0)
        for hh in range(heads):
            hs = slice(hh * hd, (hh + 1) * hd)
            bias = _moba_gate_bias(kmean_ref.at[:, hs], qt_ref[hs, :], own_blk, MOBA_MASK_VALUE)
            bias_ref[hh] = jnp.where(blk == own_blk, 0.0, bias).astype(bf16)

    @pl.when(i % 2 == 0)
    def _():
        step(0, 1)

    @pl.when(i % 2 == 1)
    def _():
        step(1, 0)


def _moba_in(x, mod, ng, w_in, qg, kg, tables):
    s, d = x.shape
    heads = MOBA_HEADS
    d_attn = w_in.shape[1] // 3
    hd = d_attn // heads
    tm = ROW_TILE
    nb = s // MOBA_BLOCK
    n = s // tm
    assert nb <= hd and nb % SUBLANES == 0 and SUBLANES % (tm // MOBA_BLOCK) == 0
    cos_t, sin_t, cos_tt, sin_tt = tables
    cur = lambda i: jnp.minimum(i, n - 1)
    prev = lambda i: jnp.maximum(i - 1, 0)
    wqt = w_in[:, 0:d_attn].T.astype(bf16)
    wk = w_in[:, d_attn:2 * d_attn].astype(bf16)
    wvt = w_in[:, 2 * d_attn:].T.astype(bf16)
    return pl.pallas_call(
        functools.partial(_moba_in_kernel, heads=heads),
        out_shape=(jax.ShapeDtypeStruct((d_attn, s), f32), jax.ShapeDtypeStruct((s, 2 * d_attn), bf16),
                   jax.ShapeDtypeStruct((d_attn, s), bf16), jax.ShapeDtypeStruct((heads, nb, s), bf16)),
        grid=(n + 1,),
        in_specs=[pl.BlockSpec((tm, d), lambda i: (cur(i), 0)), _resident(mod.shape), _resident(ng.shape),
                  _resident(wqt.shape), _resident(wk.shape), _resident(wvt.shape), _resident((hd, 1)),
                  _resident((1, hd)),
                  pl.BlockSpec((tm, hd), lambda i: (prev(i), 0)), pl.BlockSpec((tm, hd), lambda i: (prev(i), 0)),
                  pl.BlockSpec((hd, tm), lambda i: (0, prev(i))), pl.BlockSpec((hd, tm), lambda i: (0, prev(i)))],
        out_specs=(pl.BlockSpec((d_attn, tm), lambda i: (0, prev(i))),
                   pl.BlockSpec((tm, 2 * d_attn), lambda i: (prev(i), 0)),
                   pl.BlockSpec((d_attn, tm), lambda i: (0, cur(i))),
                   pl.BlockSpec((heads, nb, tm), lambda i: (0, 0, prev(i)))),
        scratch_shapes=[pltpu.VMEM((2, d_attn, tm), f32), pltpu.VMEM((2, tm, d_attn), f32),
                        pltpu.VMEM((nb, d_attn), f32)],
        compiler_params=_params("arbitrary"),
        name="moba_in_proj",
    )(x, mod, ng, wqt, wk, wvt, qg.reshape(hd, 1), kg.reshape(1, hd), cos_t, sin_t, cos_tt, sin_tt)


def _moba_gate_bias(kmean_ref, qt, own_blk, off_value):
    nb = kmean_ref.shape[0]
    tq = qt.shape[1]
    neg = jnp.float32(-jnp.inf)
    km = kmean_ref[...]
    km_hi = km.astype(bf16)
    km_lo = (km - km_hi.astype(f32)).astype(bf16)
    qt_hi = qt.astype(bf16)
    qt_lo = (qt - qt_hi.astype(f32)).astype(bf16)
    gate = _dot(km_hi, qt_hi) + _dot(km_lo, qt_hi) + _dot(km_hi, qt_lo)
    blk = lax.broadcasted_iota(jnp.int32, (nb, tq), 0)
    gate = jnp.where(blk < own_blk, gate, neg)
    bias = jnp.full((nb, tq), off_value, f32)
    for _ in range(MOBA_TOPK):
        top = jnp.max(gate, axis=0, keepdims=True)
        first = jnp.min(jnp.where(gate == top, blk, nb), axis=0, keepdims=True)
        first = jnp.where(top > neg, first, nb)
        pick = blk == first
        bias = jnp.where(pick, 0.0, bias)
        gate = jnp.where(pick, neg, gate)
    return bias


def _causal_own_scores(k_ref, qb, blk):
    bs = MOBA_BLOCK
    hd = qb.shape[0]
    st = _dot(k_ref[pl.ds(pl.multiple_of(blk * bs, bs), bs), 0:hd], qb)
    key_pos = lax.broadcasted_iota(jnp.int32, (bs, bs), 0)
    qry_pos = lax.broadcasted_iota(jnp.int32, (bs, bs), 1)
    return jnp.where(key_pos <= qry_pos, st, jnp.float32(-jnp.inf))


def _moba_attn_fast_kernel(qt_ref, k_ref, vt_ref, bias_ref, o_ref, acc_ref, l_ref, qaug_ref):
    hd, tq = qt_ref.shape
    bs = MOBA_BLOCK
    nt = tq // bs
    nb = bias_ref.shape[0]
    first = pl.program_id(1) * nt
    qaug_ref[0:hd, :] = (qt_ref[...] * (hd ** -0.5 * LOG2E)).astype(bf16)
    qaug_ref[hd:hd + nb, :] = bias_ref[...]
    qaug_ref[hd + nb:2 * hd, :] = jnp.zeros((hd - nb, tq), bf16)
    acc_ref[...] = jnp.zeros_like(acc_ref)
    l_ref[...] = jnp.zeros_like(l_ref)

    def group(g, nblk, causal):
        gk = nblk * bs
        off = pl.multiple_of(g * gk, gk)
        sg = _dot(k_ref[pl.ds(off, gk), :], qaug_ref[...])
        if causal:
            key_pos = off + lax.broadcasted_iota(jnp.int32, (gk, tq), 0)
            qry_pos = first * bs + lax.broadcasted_iota(jnp.int32, (gk, tq), 1)
            sg = jnp.where(key_pos <= qry_pos, sg, jnp.float32(-jnp.inf))
        pg = jnp.exp2(sg)
        l_ref[...] += jnp.sum(pg, axis=0, keepdims=True)
        acc_ref[...] += _dot(vt_ref[:, pl.ds(off, gk)], pg.astype(bf16))

    def past_group(g, carry):
        group(g, MOBA_GROUP, False)
        return carry

    def own_group(g, carry):
        group(g, nt, True)
        return carry

    n_past = first // MOBA_GROUP

    done = 0
    for unroll in (4, 2, 1):
        def past_groups(u, carry, unroll=unroll, done=done):
            for r in range(unroll):
                group(done + u * unroll + r, MOBA_GROUP, False)
            return carry

        steps = (n_past - done) // unroll
        lax.fori_loop(0, steps, past_groups, 0)
        done = done + steps * unroll
    fills_group = (first + nt) % MOBA_GROUP == 0

    @pl.when(fills_group)
    def _():
        group(n_past, MOBA_GROUP, True)

    @pl.when(jnp.logical_not(fills_group))
    def _():
        lax.fori_loop(n_past * (MOBA_GROUP // nt), first // nt + 1, own_group, 0)

    o_ref[...] = (acc_ref[...] / l_ref[...]).T.astype(bf16)


def _moba_attn_general_kernel(qt_ref, k_ref, vt_ref, sel_ref, o_ref, acc_ref, l_ref, m_ref, bias_ref):
    hd, tq = qt_ref.shape
    bs = MOBA_BLOCK
    nt = tq // bs
    for a in range(nt):
        ls = slice(a * bs, (a + 1) * bs)
        i = pl.program_id(1) * nt + a
        bias_ref[...] = jnp.where(sel_ref[:, ls].astype(f32) == 0.0, 0.0, -jnp.inf)
        qb = (qt_ref[:, ls] * (hd ** -0.5 * LOG2E)).astype(bf16)
        st = _causal_own_scores(k_ref, qb, i)
        m0 = jnp.max(st, axis=0, keepdims=True)
        p = jnp.exp2(st - m0)
        m_ref[...] = m0
        l_ref[:, ls] = jnp.sum(p, axis=0, keepdims=True)
        own = pl.multiple_of(i * bs, bs)
        acc_ref[:, ls] = _dot(vt_ref[:, pl.ds(own, bs)], p.astype(bf16))

        def past_block(j, carry, ls=ls, qb=qb):
            off = pl.multiple_of(j * bs, bs)
            sj = _dot(k_ref[pl.ds(off, bs), 0:hd], qb) + bias_ref[pl.ds(j, 1), :]
            m_old = m_ref[...]
            m_new = jnp.maximum(m_old, jnp.max(sj, axis=0, keepdims=True))
            alpha = jnp.exp2(m_old - m_new)
            pj = jnp.exp2(sj - m_new)
            l_ref[:, ls] = alpha * l_ref[:, ls] + jnp.sum(pj, axis=0, keepdims=True)
            acc_ref[:, ls] = alpha * acc_ref[:, ls] + _dot(vt_ref[:, pl.ds(off, bs)], pj.astype(bf16))
            m_ref[...] = m_new
            return carry

        lax.fori_loop(0, i, past_block, 0)
        o_ref[ls, :] = (acc_ref[:, ls] / l_ref[:, ls]).T.astype(bf16)


def _moba_attn(qt, k, vt, bias, qg, kg):
    d_attn, s = qt.shape
    heads = MOBA_HEADS
    hd = d_attn // heads
    tq = MOBA_QTILE
    bs = MOBA_BLOCK
    nb = s // bs
    assert nb % MOBA_GROUP == 0 and s % tq == 0 and tq % bs == 0 and MOBA_GROUP % (tq // bs) == 0

    def call(body, scratch):
        return pl.pallas_call(
            body,
            out_shape=jax.ShapeDtypeStruct((s, d_attn), bf16),
            grid=(heads, s // tq),
            in_specs=[
                pl.BlockSpec((hd, tq), lambda h, t: (h, t)),
                pl.BlockSpec((s, 2 * hd), lambda h, t: (0, h)),
                pl.BlockSpec((hd, s), lambda h, t: (h, 0)),
                pl.BlockSpec((None, nb, tq), lambda h, t: (h, 0, t)),
            ],
            out_specs=pl.BlockSpec((tq, hd), lambda h, t: (t, h)),
            scratch_shapes=scratch,
            compiler_params=_params("arbitrary", "arbitrary"),
            name=body.__name__.strip("_"),
        )(qt, k, vt, bias)

    stats = [pltpu.VMEM((hd, tq), f32), pltpu.VMEM((1, tq), f32)]
    fast = lambda: call(_moba_attn_fast_kernel, stats + [pltpu.VMEM((2 * hd, tq), bf16)])
    general = lambda: call(_moba_attn_general_kernel,
                           stats + [pltpu.VMEM((1, bs), f32), pltpu.VMEM((nb, bs), f32)])
    bound = 1.02 * hd ** 0.5 * LOG2E * jnp.max(jnp.abs(qg)) * jnp.max(jnp.abs(kg))
    return lax.cond(bound <= MOBA_FAST_RANGE, fast, general)


def _out_mlp_kernel(a_ref, x_ref, mod_ref, ng_ref, wo_ref, w1_ref, w2_ref, o_ref, acc_ref):
    x1 = x_ref[...] + mod_ref[2:3, :] * _dot(a_ref[...], wo_ref[...].astype(bf16))
    h = _norm_mod(x1, ng_ref[...], mod_ref[4:5, :], mod_ref[3:4, :]).astype(bf16)
    d_ff = w1_ref.shape[1]
    for c in range(d_ff // FF_CHUNK):
        cs = slice(c * FF_CHUNK, (c + 1) * FF_CHUNK)
        u = jnp.maximum(_dot(h, w1_ref[:, cs].astype(bf16)), 0.0)
        part = _dot((u * u).astype(bf16), w2_ref[cs, :].astype(bf16))
        if c == 0:
            acc_ref[...] = part
        else:
            acc_ref[...] += part
    o_ref[...] = x1 + mod_ref[5:6, :] * acc_ref[...]


def _out_mlp(a, x, mod, ng, w_out_all, mixer_layer, w1_all, w2_all, layer):
    s, d = x.shape
    tm = ROW_TILE
    row = lambda n: pl.BlockSpec((tm, n), lambda i: (i, 0))
    return pl.pallas_call(
        _out_mlp_kernel,
        out_shape=jax.ShapeDtypeStruct((s, d), f32),
        grid=(s // tm,),
        in_specs=[row(a.shape[1]), row(d), _resident(mod.shape), _resident(ng.shape),
                  _resident_layer(w_out_all.shape, mixer_layer), _resident_layer(w1_all.shape, layer),
                  _resident_layer(w2_all.shape, layer)],
        out_specs=row(d),
        scratch_shapes=[pltpu.VMEM((tm, d), f32)],
        compiler_params=_params("arbitrary"),
        name="out_proj_mlp",
    )(a, x, mod, ng, w_out_all, w1_all, w2_all)


def kernel(x, c, positions, ada_w, ada_b, norm_mix_g, norm_mlp_g, gla_w_in, gla_w_gate_up, gla_b_gate,
           gla_onorm_g, gla_w_out, moba_w_in, moba_q_norm_g, moba_k_norm_g, moba_w_out, mlp_w1, mlp_w2):
    b, s, d = x.shape
    assert b == 1 and s % ROW_TILE == 0 and s % MOBA_BLOCK == 0
    depth = ada_w.shape[0]
    xs = x.reshape(s, d)
    mod = _ada_mod(c, ada_w, ada_b)
    moba_hd = moba_w_out.shape[1] // MOBA_HEADS
    tables = _rope_tables(positions, moba_hd)
    for i in range(depth):
        j = i // 2
        ng = norm_mix_g[i].reshape(1, d)
        if i % 2 == 0:
            q, k, v, r, g = _gla_in(xs, mod[i], ng, gla_w_in, j, gla_w_gate_up[j], gla_b_gate[j])
            a = _gla_core(q, k, g, v, r, gla_onorm_g[j])
            w_out = gla_w_out
        else:
            q, k, vt, bias = _moba_in(xs, mod[i], ng, moba_w_in[j], moba_q_norm_g[j], moba_k_norm_g[j], tables)
            a = _moba_attn(q, k, vt, bias, moba_q_norm_g[j], moba_k_norm_g[j])
            w_out = moba_w_out
        xs = _out_mlp(a, xs, mod[i], norm_mlp_g[i].reshape(1, d), w_out, j, mlp_w1, mlp_w2, i)
    return xs.reshape(b, s, d)
```
